```python
import math
import jax, jax.numpy as jnp
from jax import lax
import numpy as np

D_MODEL = 4096
BATCH = 2
SEQ = 4096
DEPTH = 2

CHUNK = 64
N_MIXERS = 2
N_A = (DEPTH + N_MIXERS - 1) // N_MIXERS
N_B = DEPTH // N_MIXERS
EPS = 1e-6
N_MOD = 6
SGU_BLOCK = 128
SGU_HIDDEN = 3 * D_MODEL
SGU_GROUPS = 16
SGU_GROUP_DIM = SGU_HIDDEN // SGU_GROUPS
MLA_HEADS = 64
Q_LORA = 1536
KV_LORA = 512
QK_NOPE = 128
QK_ROPE = 64
V_DIM = 128
ROPE_THETA = 10000.0
Q_BLOCK = 128
D_FF = 10752
CONV_W = 3

kernel_name = 'hybrid_gmlp_mla_convffn_adaln'


def _rmsnorm(x, g=None):
    xf = x.astype(jnp.float32)
    y = xf * lax.rsqrt(jnp.mean(xf * xf, axis=-1, keepdims=True) + EPS)
    if g is not None:
        y = y * g.astype(jnp.float32)
    return y.astype(x.dtype)


def _layernorm(x, g, b):
    xf = x.astype(jnp.float32)
    mu = jnp.mean(xf, axis=-1, keepdims=True)
    var = jnp.mean(jnp.square(xf - mu), axis=-1, keepdims=True)
    y = (xf - mu) * lax.rsqrt(var + EPS) * g.astype(jnp.float32) + b.astype(jnp.float32)
    return y.astype(x.dtype)


def _modulate(h, shift, scale):
    return h * (1 + scale[:, None, :]) + shift[:, None, :]


def _rope_cos_sin(positions):
    inv_freq = ROPE_THETA ** (-jnp.arange(0, QK_ROPE, 2, dtype=jnp.float32) / QK_ROPE)
    ang = positions.astype(jnp.float32)[..., None] * inv_freq
    return jnp.cos(ang), jnp.sin(ang)


def _apply_rope(x, cos, sin):
    x1, x2 = jnp.split(x.astype(jnp.float32), 2, axis=-1)
    return jnp.concatenate([x1 * cos - x2 * sin, x1 * sin + x2 * cos], axis=-1).astype(x.dtype)


def _sgu_mixer(h, w_in, b_in, ln_g, ln_b, w_s, b_s, w_out):
    bsz, seq, _ = h.shape
    z = jax.nn.gelu(h @ w_in + b_in, approximate=False)
    u, v = jnp.split(z, 2, axis=-1)
    v = _layernorm(v, ln_g, ln_b)
    t = jnp.arange(SGU_BLOCK)
    mask = (t[None, :] // CHUNK) <= (t[:, None] // CHUNK)
    w = jnp.where(mask[None], w_s, jnp.zeros_like(w_s))
    vb = v.reshape(bsz, seq // SGU_BLOCK, SGU_BLOCK, SGU_GROUPS, SGU_GROUP_DIM)
    mixed = jnp.einsum('gts,bnsgc->bntgc', w, vb) + b_s.T[None, None, :, :, None]
    return (u * mixed.reshape(bsz, seq, SGU_HIDDEN)) @ w_out


def _mla(h, positions, w_in, g_q, g_kv, w_uq, w_ukv, w_o):
    bsz, seq, _ = h.shape
    proj = h @ w_in
    c_q = _rmsnorm(proj[..., :Q_LORA], g_q)
    c_kv = _rmsnorm(proj[..., Q_LORA:Q_LORA + KV_LORA], g_kv)
    k_rope = proj[..., Q_LORA + KV_LORA:]
    q = (c_q @ w_uq).reshape(bsz, seq, MLA_HEADS, QK_NOPE + QK_ROPE)
    kv = (c_kv @ w_ukv).reshape(bsz, seq, MLA_HEADS, QK_NOPE + V_DIM)
    k_nope, v = kv[..., :QK_NOPE], kv[..., QK_NOPE:]
    cos, sin = _rope_cos_sin(positions)
    q_nope = q[..., :QK_NOPE]
    q_rope = _apply_rope(q[..., QK_NOPE:], cos[:, :, None, :], sin[:, :, None, :])
    k_rope = _apply_rope(k_rope, cos, sin)
    n_blk = seq // Q_BLOCK
    qn_blocks = q_nope.reshape(bsz, n_blk, Q_BLOCK, MLA_HEADS, QK_NOPE).transpose(1, 0, 2, 3, 4)
    qr_blocks = q_rope.reshape(bsz, n_blk, Q_BLOCK, MLA_HEADS, QK_ROPE).transpose(1, 0, 2, 3, 4)
    starts = jnp.arange(n_blk, dtype=jnp.int32) * Q_BLOCK
    key_chunk = jnp.arange(seq) // CHUNK
    scale = (QK_NOPE + QK_ROPE) ** -0.5

    def attend(args):
        qn, qr, start = args
        s = (jnp.einsum('bqhd,bkhd->bhqk', qn, k_nope)
             + jnp.einsum('bqhr,bkr->bhqk', qr, k_rope)).astype(jnp.float32) * scale
        q_chunk = (start + jnp.arange(Q_BLOCK)) // CHUNK
        allowed = key_chunk[None, :] <= q_chunk[:, None]
        s = jnp.where(allowed[None, None], s, -jnp.inf)
        p = jax.nn.softmax(s, axis=-1).astype(v.dtype)
        return jnp.einsum('bhqk,bkhd->bqhd', p, v)

    o = lax.map(attend, (qn_blocks, qr_blocks, starts))
    o = o.transpose(1, 0, 2, 3, 4).reshape(bsz, seq, MLA_HEADS * V_DIM)
    return o @ w_o


def _conv_ffn(h, w_up, conv_w, conv_b, w_down):
    seq = h.shape[1]
    a = h @ w_up
    ap = jnp.pad(a, ((0, 0), (CONV_W - 1, 0), (0, 0)))
    a = conv_b + sum(conv_w[k] * ap[:, k:k + seq] for k in range(CONV_W))
    g, u = jnp.split(a, 2, axis=-1)
    return (jax.nn.silu(g) * u) @ w_down


def setup_inputs(seed: int = 0) -> dict:
    key = jax.random.key(seed)
    ks = jax.random.split(key, 24)

    def nrm(k, shape, fan_in, gain=1.0):
        return jax.random.normal(k, shape, jnp.float32) * (gain * fan_in ** -0.5)

    def small(k, shape):
        return 0.01 * jax.random.normal(k, shape, jnp.float32)

    x = jax.random.normal(ks[0], (BATCH, SEQ, D_MODEL), jnp.float32)
    c = jax.random.normal(ks[1], (BATCH, D_MODEL), jnp.float32)
    offset = jax.random.randint(ks[2], (BATCH, 1), 0, 1024, dtype=jnp.int32)
    positions = (offset + jnp.arange(SEQ, dtype=jnp.int32)[None, :]).astype(jnp.int32)
    return {
        'x': x,
        'c': c,
        'positions': positions,
        'ada_w': nrm(ks[3], (DEPTH, D_MODEL, N_MOD * D_MODEL), D_MODEL, 0.5),
        'ada_b': small(ks[4], (DEPTH, N_MOD * D_MODEL)),
        'sgu_w_in': nrm(ks[5], (N_A, D_MODEL, 2 * SGU_HIDDEN), D_MODEL),
        'sgu_b_in': small(ks[6], (N_A, 2 * SGU_HIDDEN)),
        'sgu_ln_g': 1.0 + small(ks[7], (N_A, SGU_HIDDEN)),
        'sgu_ln_b': small(ks[8], (N_A, SGU_HIDDEN)),
        'sgu_w_s': nrm(ks[9], (N_A, SGU_GROUPS, SGU_BLOCK, SGU_BLOCK), SGU_BLOCK),
        'sgu_b_s': 1.0 + small(ks[10], (N_A, SGU_GROUPS, SGU_BLOCK)),
        'sgu_w_out': nrm(ks[11], (N_A, SGU_HIDDEN, D_MODEL), SGU_HIDDEN),
        'mla_w_in': nrm(ks[12], (N_B, D_MODEL, Q_LORA + KV_LORA + QK_ROPE), D_MODEL),
        'mla_g_q': 1.0 + small(ks[13], (N_B, Q_LORA)),
        'mla_g_kv': 1.0 + small(ks[14], (N_B, KV_LORA)),
        'mla_w_uq': nrm(ks[15], (N_B, Q_LORA, MLA_HEADS * (QK_NOPE + QK_ROPE)), Q_LORA),
        'mla_w_ukv': nrm(ks[16], (N_B, KV_LORA, MLA_HEADS * (QK_NOPE + V_DIM)), KV_LORA),
        'mla_w_o': nrm(ks[17], (N_B, MLA_HEADS * V_DIM, D_MODEL), MLA_HEADS * V_DIM),
        'ffn_w_up': nrm(ks[18], (DEPTH, D_MODEL, 2 * D_FF), D_MODEL),
        'ffn_conv_w': nrm(ks[19], (DEPTH, CONV_W, 2 * D_FF), CONV_W),
        'ffn_conv_b': small(ks[20], (DEPTH, 2 * D_FF)),
        'ffn_w_down': nrm(ks[21], (DEPTH, D_FF, D_MODEL), D_FF),
        'norm_g': 1.0 + small(ks[22], (D_MODEL,)),
    }


def reference(x, c, positions, ada_w, ada_b, sgu_w_in, sgu_b_in, sgu_ln_g, sgu_ln_b, sgu_w_s, sgu_b_s, sgu_w_out, mla_w_in, mla_g_q, mla_g_kv, mla_w_uq, mla_w_ukv, mla_w_o, ffn_w_up, ffn_conv_w, ffn_conv_b, ffn_w_down, norm_g):
    c_act = jax.nn.silu(c)
    for i in range(DEPTH):
        mod = c_act @ ada_w[i] + ada_b[i]
        sh_m, sc_m, g_m, sh_f, sc_f, g_f = jnp.split(mod, N_MOD, axis=-1)
        h = _modulate(_rmsnorm(x), sh_m, sc_m)
        j = i // N_MIXERS
        if i % N_MIXERS == 0:
            y = _sgu_mixer(h, sgu_w_in[j], sgu_b_in[j], sgu_ln_g[j], sgu_ln_b[j],
                           sgu_w_s[j], sgu_b_s[j], sgu_w_out[j])
        else:
            y = _mla(h, positions, mla_w_in[j], mla_g_q[j], mla_g_kv[j],
                     mla_w_uq[j], mla_w_ukv[j], mla_w_o[j])
        x = x + g_m[:, None, :] * y
        h = _modulate(_rmsnorm(x), sh_f, sc_f)
        x = x + g_f[:, None, :] * _conv_ffn(h, ffn_w_up[i], ffn_conv_w[i], ffn_conv_b[i], ffn_w_down[i])
    return _rmsnorm(x, norm_g)
```

```python
import functools
import math

import jax
import jax.numpy as jnp
from jax import lax
from jax.experimental import pallas as pl
from jax.experimental.pallas import tpu as pltpu

CHUNK = 64
EPS = 1e-6
N_MOD = 6
SGU_BLOCK = 128
SGU_GROUPS = 16
MLA_HEADS = 64
Q_LORA = 1536
KV_LORA = 512
QK_NOPE = 128
QK_ROPE = 64
V_DIM = 128
ROPE_THETA = 10000.0
CONV_W = 3

LANES = 128
SUBLANES = 8
VMEM_LIMIT_BYTES = 56 * 1024 * 1024

F32 = jnp.float32
BF16 = jnp.bfloat16
NEG_BIG = -1e30


def _params(*semantics):
    return pltpu.CompilerParams(dimension_semantics=semantics,
                                vmem_limit_bytes=VMEM_LIMIT_BYTES)


def _tile(dim, want):
    t = min(dim, want)
    while dim % t:
        t -= 1
    return t


def _ada_kernel(c_ref, w_ref, b_ref, o_ref):
    c = c_ref[...]
    c_act = (c / (1.0 + jnp.exp(-c))).astype(BF16)
    w = w_ref[...].astype(BF16)
    o_ref[...] = jnp.dot(c_act, w, preferred_element_type=F32) + b_ref[...]


def _ada_mod(c_pad, ada_w, ada_b):
    depth, d, n = ada_w.shape
    rows = c_pad.shape[0]
    bn = _tile(n, 512)
    return pl.pallas_call(
        _ada_kernel,
        grid=(depth, n // bn),
        in_specs=[
            pl.BlockSpec((rows, d), lambda i, j: (0, 0)),
            pl.BlockSpec((None, d, bn), lambda i, j: (i, 0, j)),
            pl.BlockSpec((None, 1, bn), lambda i, j: (i, 0, j)),
        ],
        out_specs=pl.BlockSpec((None, rows, bn), lambda i, j: (i, 0, j)),
        out_shape=jax.ShapeDtypeStruct((depth, rows, n), F32),
        compiler_params=_params("arbitrary", "arbitrary"),
        name="ada_mod",
    )(c_pad, ada_w, ada_b.reshape(depth, 1, n))


def _normmod_kernel(x_ref, mod_ref, o_ref, *, shift_row, scale_row):
    x = x_ref[...]
    ms = jnp.mean(x * x, axis=-1, keepdims=True)
    y = x * lax.rsqrt(ms + EPS)
    sh = mod_ref[shift_row:shift_row + 1, :]
    sc = mod_ref[scale_row:scale_row + 1, :]
    o_ref[...] = (y * (1.0 + sc) + sh).astype(BF16)


def _norm_mod(x2, mod4, layer, shift_row, scale_row, seq):
    m, d = x2.shape
    ts = _tile(seq, 256)
    per_seq = seq // ts
    return pl.pallas_call(
        functools.partial(_normmod_kernel, shift_row=shift_row, scale_row=scale_row),
        grid=(m // ts,),
        in_specs=[
            pl.BlockSpec((ts, d), lambda i: (i, 0)),
            pl.BlockSpec((None, None, N_MOD, d), lambda i: (layer, i // per_seq, 0, 0)),
        ],
        out_specs=pl.BlockSpec((ts, d), lambda i: (i, 0)),
        out_shape=jax.ShapeDtypeStruct((m, d), BF16),
        compiler_params=_params("arbitrary"),
        name="norm_mod",
    )(x2, mod4)


def _sgu_in_kernel(h_ref, w_ref, b_ref, z_ref):
    a = jnp.dot(h_ref[...], w_ref[...], preferred_element_type=F32) + b_ref[...]
    z = 0.5 * a * (1.0 + lax.erf(a * math.sqrt(0.5)))
    z_ref[...] = z.astype(BF16)


def _sgu_in(h, w, b):
    m, d = h.shape
    n = w.shape[1]
    bm = _tile(m, 1024)
    bn = _tile(n, 1024)
    return pl.pallas_call(
        _sgu_in_kernel,
        grid=(n // bn, m // bm),
        in_specs=[
            pl.BlockSpec((bm, d), lambda j, i: (i, 0)),
            pl.BlockSpec((d, bn), lambda j, i: (0, j)),
            pl.BlockSpec((1, bn), lambda j, i: (0, j)),
        ],
        out_specs=pl.BlockSpec((bm, bn), lambda j, i: (i, j)),
        out_shape=jax.ShapeDtypeStruct((m, n), BF16),
        compiler_params=_params("arbitrary", "arbitrary"),
        name="sgu_in",
    )(h, w, b.reshape(1, n))


def _sgu_gate_kernel(u_ref, v_ref, g_ref, b_ref, ws_ref, bst_ref, o_ref, *, groups):
    blk, hidden = v_ref.shape
    gd = hidden // groups
    s1 = jnp.zeros((blk, 1), F32)
    s2 = jnp.zeros((blk, 1), F32)
    for g in range(groups):
        vg = v_ref[:, g * gd:(g + 1) * gd].astype(F32)
        s1 = s1 + jnp.sum(vg, axis=-1, keepdims=True)
    mu = s1 * (1.0 / hidden)
    for g in range(groups):
        vg = v_ref[:, g * gd:(g + 1) * gd].astype(F32) - mu
        s2 = s2 + jnp.sum(vg * vg, axis=-1, keepdims=True)
    rstd = lax.rsqrt(s2 * (1.0 / hidden) + EPS)
    t_out = lax.broadcasted_iota(jnp.int32, (blk, blk), 0) // CHUNK
    s_in = lax.broadcasted_iota(jnp.int32, (blk, blk), 1) // CHUNK
    causal = s_in <= t_out
    for g in range(groups):
        sl = slice(g * gd, (g + 1) * gd)
        vn = (v_ref[:, sl].astype(F32) - mu) * rstd * g_ref[:, sl] + b_ref[:, sl]
        w = jnp.where(causal, ws_ref[g], 0.0).astype(BF16)
        mixed = jnp.dot(w, vn.astype(BF16), preferred_element_type=F32) + bst_ref[:, g:g + 1]
        o_ref[:, sl] = (u_ref[:, sl].astype(F32) * mixed).astype(BF16)


def _sgu_gate(z, ln_g, ln_b, w_s, b_s):
    m, two_h = z.shape
    hidden = two_h // 2
    groups = w_s.shape[0]
    blk = SGU_BLOCK
    return pl.pallas_call(
        functools.partial(_sgu_gate_kernel, groups=groups),
        grid=(m // blk,),
        in_specs=[
            pl.BlockSpec((blk, hidden), lambda i: (i, 0)),
            pl.BlockSpec((blk, hidden), lambda i: (i, 1)),
            pl.BlockSpec((1, hidden), lambda i: (0, 0)),
            pl.BlockSpec((1, hidden), lambda i: (0, 0)),
            pl.BlockSpec((groups, blk, blk), lambda i: (0, 0, 0)),
            pl.BlockSpec((blk, groups), lambda i: (0, 0)),
        ],
        out_specs=pl.BlockSpec((blk, hidden), lambda i: (i, 0)),
        out_shape=jax.ShapeDtypeStruct((m, hidden), BF16),
        compiler_params=_params("arbitrary"),
        name="sgu_gate",
    )(z, z, ln_g.reshape(1, hidden), ln_b.reshape(1, hidden), w_s, b_s.T)


def _mm_res_kernel(a_ref, w_ref, x_ref, mod_ref, o_ref, *, gate_row, nk):
    k = pl.program_id(1)
    part = jnp.dot(a_ref[...], w_ref[...], preferred_element_type=F32)

    @pl.when(k == 0)
    def _():
        o_ref[...] = part

    @pl.when(k > 0)
    def _():
        o_ref[...] += part

    @pl.when(k == nk - 1)
    def _():
        gate = mod_ref[gate_row:gate_row + 1, :]
        o_ref[...] = x_ref[...] + gate * o_ref[...]


def _mm_residual(a, w, x2, mod4, layer, gate_row, seq):
    m, kdim = a.shape
    d = w.shape[1]
    bm = _tile(seq, 512)
    tk = _tile(kdim, 512)
    nk = kdim // tk
    per_seq = seq // bm
    return pl.pallas_call(
        functools.partial(_mm_res_kernel, gate_row=gate_row, nk=nk),
        grid=(m // bm, nk),
        in_specs=[
            pl.BlockSpec((bm, tk), lambda i, k: (i, k)),
            pl.BlockSpec((tk, d), lambda i, k: (k, 0)),
            pl.BlockSpec((bm, d), lambda i, k: (i, 0)),
            pl.BlockSpec((None, None, N_MOD, d), lambda i, k: (layer, i // per_seq, 0, 0)),
        ],
        out_specs=pl.BlockSpec((bm, d), lambda i, k: (i, 0)),
        out_shape=jax.ShapeDtypeStruct((m, d), F32),
        compiler_params=_params("arbitrary", "arbitrary"),
        name="mm_residual",
    )(a, w, x2, mod4)


def _ffn_up_kernel(h_ref, wg_ref, wu_ref, cwg_ref, cwu_ref, cbg_ref, cbu_ref, o_ref,
                   halo_g, halo_u, *, per_seq):
    i = pl.program_id(1)
    first = (i % per_seq) == 0
    h = h_ref[...]
    bm = h.shape[0]

    @pl.when(first)
    def _():
        halo_g[...] = jnp.zeros_like(halo_g)
        halo_u[...] = jnp.zeros_like(halo_u)

    def conv(a, halo, cw_ref, cb_ref):
        prev = halo[...]
        halo[...] = a[bm - SUBLANES:, :]
        row = lax.broadcasted_iota(jnp.int32, a.shape, 0)
        last1 = prev[SUBLANES - 1:SUBLANES, :]
        last2 = prev[SUBLANES - 2:SUBLANES - 1, :]
        a1 = jnp.where(row == 0, last1, pltpu.roll(a, 1, axis=0))
        a2 = jnp.where(row == 0, last2, jnp.where(row == 1, last1, pltpu.roll(a, 2, axis=0)))
        return cb_ref[...] + cw_ref[0:1, :] * a2 + cw_ref[1:2, :] * a1 + cw_ref[2:3, :] * a

    g = conv(jnp.dot(h, wg_ref[...], preferred_element_type=F32), halo_g, cwg_ref, cbg_ref)
    u = conv(jnp.dot(h, wu_ref[...], preferred_element_type=F32), halo_u, cwu_ref, cbu_ref)
    o_ref[...] = (g / (1.0 + jnp.exp(-g)) * u).astype(BF16)


def _ffn_up(h, w_up, conv_w, conv_b, seq):
    m, d = h.shape
    ff = w_up.shape[1] // 2
    bm = _tile(seq, 1024)
    bn = _tile(ff, 512)
    nf = ff // bn
    per_seq = seq // bm
    cb = conv_b.reshape(1, 2 * ff)
    return pl.pallas_call(
        functools.partial(_ffn_up_kernel, per_seq=per_seq),
        grid=(nf, m // bm),
        in_specs=[
            pl.BlockSpec((bm, d), lambda j, i: (i, 0)),
            pl.BlockSpec((d, bn), lambda j, i: (0, j)),
            pl.BlockSpec((d, bn), lambda j, i: (0, j + nf)),
            pl.BlockSpec((CONV_W, bn), lambda j, i: (0, j)),
            pl.BlockSpec((CONV_W, bn), lambda j, i: (0, j + nf)),
            pl.BlockSpec((1, bn), lambda j, i: (0, j)),
            pl.BlockSpec((1, bn), lambda j, i: (0, j + nf)),
        ],
        out_specs=pl.BlockSpec((bm, bn), lambda j, i: (i, j)),
        out_shape=jax.ShapeDtypeStruct((m, ff), BF16),
        scratch_shapes=[pltpu.VMEM((SUBLANES, bn), F32), pltpu.VMEM((SUBLANES, bn), F32)],
        compiler_params=_params("arbitrary", "arbitrary"),
        name="ffn_up",
    )(h, w_up, w_up, conv_w, conv_w, cb, cb)


def _rope_table_kernel(pos_ref, freq_ref, cos_ref, sin_ref):
    ang = pos_ref[...].astype(F32) * freq_ref[...]
    lane = lax.broadcasted_iota(jnp.int32, ang.shape, 1)
    half = QK_ROPE // 2
    live = lane < QK_ROPE
    cos_ref[...] = jnp.where(live, jnp.cos(ang), 0.0)
    sn = jnp.sin(ang)
    sin_ref[...] = jnp.where(live, jnp.where(lane < half, -sn, sn), 0.0)


def _rope_tables(pos_col, freq_row):
    m = pos_col.shape[0]
    bm = _tile(m, 1024)
    return pl.pallas_call(
        _rope_table_kernel,
        grid=(m // bm,),
        in_specs=[pl.BlockSpec((bm, 1), lambda i: (i, 0)),
                  pl.BlockSpec((1, LANES), lambda i: (0, 0))],
        out_specs=[pl.BlockSpec((bm, LANES), lambda i: (i, 0)),
                   pl.BlockSpec((bm, LANES), lambda i: (i, 0))],
        out_shape=[jax.ShapeDtypeStruct((m, LANES), F32)] * 2,
        compiler_params=_params("arbitrary"),
        name="rope_tables",
    )(pos_col, freq_row)


def _rope_128(x, cos_t, sin_t):
    half = QK_ROPE // 2
    lane = lax.broadcasted_iota(jnp.int32, x.shape, 1)
    swapped = jnp.where(lane < half,
                        pltpu.roll(x, LANES - half, axis=1),
                        pltpu.roll(x, half, axis=1))
    return x * cos_t + swapped * sin_t


def _mla_in_kernel(h_ref, w_ref, gq_ref, gkv_ref, cos_ref, sin_ref,
                   cq_ref, ckv_ref, kr_ref, acc_ref, *, nk):
    k = pl.program_id(1)
    part = jnp.dot(h_ref[...], w_ref[...], preferred_element_type=F32)

    @pl.when(k == 0)
    def _():
        acc_ref[...] = part

    @pl.when(k > 0)
    def _():
        acc_ref[...] += part

    @pl.when(k == nk - 1)
    def _():
        def rms(v, g):
            return v * lax.rsqrt(jnp.mean(v * v, axis=-1, keepdims=True) + EPS) * g

        cq_ref[...] = rms(acc_ref[:, :Q_LORA], gq_ref[...]).astype(BF16)
        ckv_ref[...] = rms(acc_ref[:, Q_LORA:Q_LORA + KV_LORA], gkv_ref[...]).astype(BF16)
        kr = acc_ref[:, Q_LORA + KV_LORA:]
        kr_ref[...] = _rope_128(kr, cos_ref[...], sin_ref[...]).astype(BF16)


def _mla_in(h, w_pad, g_q, g_kv, cos_t, sin_t):
    m, d = h.shape
    n = w_pad.shape[1]
    bm = _tile(m, 512)
    tk = _tile(d, 1024)
    nk = d // tk
    return pl.pallas_call(
        functools.partial(_mla_in_kernel, nk=nk),
        grid=(m // bm, nk),
        in_specs=[
            pl.BlockSpec((bm, tk), lambda i, k: (i, k)),
            pl.BlockSpec((tk, n), lambda i, k: (k, 0)),
            pl.BlockSpec((1, Q_LORA), lambda i, k: (0, 0)),
            pl.BlockSpec((1, KV_LORA), lambda i, k: (0, 0)),
            pl.BlockSpec((bm, LANES), lambda i, k: (i, 0)),
            pl.BlockSpec((bm, LANES), lambda i, k: (i, 0)),
        ],
        out_specs=[
            pl.BlockSpec((bm, Q_LORA), lambda i, k: (i, 0)),
            pl.BlockSpec((bm, KV_LORA), lambda i, k: (i, 0)),
            pl.BlockSpec((bm, LANES), lambda i, k: (i, 0)),
        ],
        out_shape=[
            jax.ShapeDtypeStruct((m, Q_LORA), BF16),
            jax.ShapeDtypeStruct((m, KV_LORA), BF16),
            jax.ShapeDtypeStruct((m, LANES), BF16),
        ],
        scratch_shapes=[pltpu.VMEM((bm, n), F32)],
        compiler_params=_params("arbitrary", "arbitrary"),
        name="mla_in",
    )(h, w_pad, g_q.reshape(1, Q_LORA), g_kv.reshape(1, KV_LORA), cos_t, sin_t)


def _q_up_kernel(cq_ref, w_ref, cos_ref, sin_ref, q_ref, *, hb, scale):
    acc = jnp.dot(cq_ref[...], w_ref[...], preferred_element_type=F32)
    width = QK_NOPE + LANES
    cos_t = cos_ref[...]
    sin_t = sin_ref[...]
    for j in range(hb):
        nope = acc[:, j * width:j * width + QK_NOPE]
        rope = _rope_128(acc[:, j * width + QK_NOPE:(j + 1) * width], cos_t, sin_t)
        q_ref[j, :, :QK_NOPE] = (nope * scale).astype(BF16)
        q_ref[j, :, QK_NOPE:] = (rope * scale).astype(BF16)


def _q_up(c_q, w_q_pad, cos_t, sin_t, batch, seq):
    m, r = c_q.shape
    width = QK_NOPE + LANES
    heads = w_q_pad.shape[1] // width
    hb = _tile(heads, 4)
    bm = _tile(seq, 1024)
    per_seq = seq // bm
    scale = (QK_NOPE + QK_ROPE) ** -0.5
    return pl.pallas_call(
        functools.partial(_q_up_kernel, hb=hb, scale=scale),
        grid=(heads // hb, m // bm),
        in_specs=[
            pl.BlockSpec((bm, r), lambda j, i: (i, 0)),
            pl.BlockSpec((r, hb * width), lambda j, i: (0, j)),
            pl.BlockSpec((bm, LANES), lambda j, i: (i, 0)),
            pl.BlockSpec((bm, LANES), lambda j, i: (i, 0)),
        ],
        out_specs=pl.BlockSpec((None, hb, bm, width),
                               lambda j, i: (i // per_seq, j, i % per_seq, 0)),
        out_shape=jax.ShapeDtypeStruct((batch, heads, seq, width), BF16),
        compiler_params=_params("arbitrary", "arbitrary"),
        name="q_up",
    )(c_q, w_q_pad, cos_t, sin_t)


def _kv_up_kernel(ckv_ref, w_ref, kr_ref, k_ref, v_ref, *, hb):
    acc = jnp.dot(ckv_ref[...], w_ref[...], preferred_element_type=F32)
    width = QK_NOPE + V_DIM
    kr = kr_ref[...]
    for j in range(hb):
        k_ref[j, :, :QK_NOPE] = acc[:, j * width:j * width + QK_NOPE].astype(BF16)
        k_ref[j, :, QK_NOPE:] = kr
        v_ref[j] = acc[:, j * width + QK_NOPE:(j + 1) * width].astype(BF16)


def _kv_up(c_kv, w_ukv, k_rope, batch, seq):
    m, r = c_kv.shape
    width = QK_NOPE + V_DIM
    heads = w_ukv.shape[1] // width
    hb = _tile(heads, 4)
    bm = _tile(seq, 1024)
    per_seq = seq // bm
    return pl.pallas_call(
        functools.partial(_kv_up_kernel, hb=hb),
        grid=(heads // hb, m // bm),
        in_specs=[
            pl.BlockSpec((bm, r), lambda j, i: (i, 0)),
            pl.BlockSpec((r, hb * width), lambda j, i: (0, j)),
            pl.BlockSpec((bm, LANES), lambda j, i: (i, 0)),
        ],
        out_specs=[
            pl.BlockSpec((None, hb, bm, QK_NOPE + LANES),
                         lambda j, i: (i // per_seq, j, i % per_seq, 0)),
            pl.BlockSpec((None, hb, bm, V_DIM),
                         lambda j, i: (i // per_seq, j, i % per_seq, 0)),
        ],
        out_shape=[
            jax.ShapeDtypeStruct((batch, heads, seq, QK_NOPE + LANES), BF16),
            jax.ShapeDtypeStruct((batch, heads, seq, V_DIM), BF16),
        ],
        compiler_params=_params("arbitrary", "arbitrary"),
        name="kv_up",
    )(c_kv, w_ukv, k_rope)


def _attn_kernel(q_ref, k_ref, v_ref, o_ref, *, hb, tq):
    seq = q_ref.shape[1]
    nq = seq // tq
    nt_dims = (((1,), (1,)), ((), ()))
    row_chunk = lax.broadcasted_iota(jnp.int32, (tq, tq), 0) // CHUNK
    col_chunk = lax.broadcasted_iota(jnp.int32, (tq, tq), 1) // CHUNK
    diag_ok = col_chunk <= row_chunk

    for j in range(hb):
        def q_tile(qi, carry, j=j):
            q0 = pl.multiple_of(qi * tq, tq)
            q = q_ref[j, pl.ds(q0, tq), :]
            s = lax.dot_general(q, k_ref[j, pl.ds(q0, tq), :], nt_dims, preferred_element_type=F32)
            s = jnp.where(diag_ok, s, NEG_BIG)
            m0 = jnp.max(s, axis=-1, keepdims=True)
            p = jnp.exp(s - m0)
            l0 = jnp.sum(p, axis=-1, keepdims=True)
            acc0 = jnp.dot(p.astype(BF16), v_ref[j, pl.ds(q0, tq), :], preferred_element_type=F32)

            def k_tile(kj, c):
                m_i, l_i, acc = c
                k0 = pl.multiple_of(kj * tq, tq)
                s = lax.dot_general(q, k_ref[j, pl.ds(k0, tq), :], nt_dims,
                                    preferred_element_type=F32)
                m_new = jnp.maximum(m_i, jnp.max(s, axis=-1, keepdims=True))
                alpha = jnp.exp(m_i - m_new)
                p = jnp.exp(s - m_new)
                l_new = alpha * l_i + jnp.sum(p, axis=-1, keepdims=True)
                acc = alpha * acc + jnp.dot(p.astype(BF16), v_ref[j, pl.ds(k0, tq), :],
                                            preferred_element_type=F32)
                return m_new, l_new, acc

            _, l_f, acc_f = lax.fori_loop(0, qi, k_tile, (m0, l0, acc0))
            o_ref[pl.ds(q0, tq), j * V_DIM:(j + 1) * V_DIM] = (acc_f / l_f).astype(BF16)
            return carry

        lax.fori_loop(0, nq, q_tile, 0)


def _attention(q, k, v):
    batch, heads, seq, width = q.shape
    hb = _tile(heads, 2)
    tq = _tile(seq, 512)
    return pl.pallas_call(
        functools.partial(_attn_kernel, hb=hb, tq=tq),
        grid=(batch, heads // hb),
        in_specs=[
            pl.BlockSpec((None, hb, seq, width), lambda b, h: (b, h, 0, 0)),
            pl.BlockSpec((None, hb, seq, width), lambda b, h: (b, h, 0, 0)),
            pl.BlockSpec((None, hb, seq, V_DIM), lambda b, h: (b, h, 0, 0)),
        ],
        out_specs=pl.BlockSpec((None, seq, hb * V_DIM), lambda b, h: (b, 0, h)),
        out_shape=jax.ShapeDtypeStruct((batch, seq, heads * V_DIM), BF16),
        compiler_params=_params("arbitrary", "arbitrary"),
        name="attention",
    )(q, k, v)


def _final_norm_kernel(x_ref, g_ref, o_ref):
    x = x_ref[...]
    o_ref[...] = x * lax.rsqrt(jnp.mean(x * x, axis=-1, keepdims=True) + EPS) * g_ref[...]


def _final_norm(x2, g):
    m, d = x2.shape
    ts = _tile(m, 256)
    return pl.pallas_call(
        _final_norm_kernel,
        grid=(m // ts,),
        in_specs=[pl.BlockSpec((ts, d), lambda i: (i, 0)),
                  pl.BlockSpec((1, d), lambda i: (0, 0))],
        out_specs=pl.BlockSpec((ts, d), lambda i: (i, 0)),
        out_shape=jax.ShapeDtypeStruct((m, d), F32),
        compiler_params=_params("arbitrary"),
        name="final_norm",
    )(x2, g.reshape(1, d))


def _pad_mla_w_in(w):
    return jnp.pad(w, ((0, 0), (0, LANES - QK_ROPE))).astype(BF16)


def _pad_w_uq(w):
    r = w.shape[0]
    heads = w.shape[1] // (QK_NOPE + QK_ROPE)
    w3 = w.reshape(r, heads, QK_NOPE + QK_ROPE)
    w3 = jnp.pad(w3, ((0, 0), (0, 0), (0, LANES - QK_ROPE)))
    return w3.reshape(r, heads * (QK_NOPE + LANES)).astype(BF16)


def kernel(x, c, positions, ada_w, ada_b, sgu_w_in, sgu_b_in, sgu_ln_g, sgu_ln_b, sgu_w_s, sgu_b_s, sgu_w_out, mla_w_in, mla_g_q, mla_g_kv, mla_w_uq, mla_w_ukv, mla_w_o, ffn_w_up, ffn_conv_w, ffn_conv_b, ffn_w_down, norm_g):
    batch, seq, d = x.shape
    depth = ada_w.shape[0]
    m = batch * seq

    c_rows = 2 * SUBLANES
    c_pad = jnp.pad(c, ((0, c_rows - batch), (0, 0)))
    mod = _ada_mod(c_pad, ada_w, ada_b)[:, :batch]
    mod4 = mod.reshape(depth, batch, N_MOD, d)

    inv_freq = ROPE_THETA ** (-jnp.arange(0, QK_ROPE, 2, dtype=F32) / QK_ROPE)
    freq_row = jnp.concatenate([inv_freq, inv_freq, jnp.zeros((LANES - QK_ROPE,), F32)]).reshape(1, LANES)
    cos_t, sin_t = _rope_tables(positions.reshape(m, 1), freq_row)

    x2 = x.reshape(m, d)
    for i in range(depth):
        j = i // 2
        h = _norm_mod(x2, mod4, i, 0, 1, seq)
        if i % 2 == 0:
            z = _sgu_in(h, sgu_w_in[j].astype(BF16), sgu_b_in[j])
            gated = _sgu_gate(z, sgu_ln_g[j], sgu_ln_b[j], sgu_w_s[j], sgu_b_s[j])
            x2 = _mm_residual(gated, sgu_w_out[j].astype(BF16), x2, mod4, i, 2, seq)
        else:
            c_q, c_kv, k_rope = _mla_in(h, _pad_mla_w_in(mla_w_in[j]), mla_g_q[j], mla_g_kv[j],
                                        cos_t, sin_t)
            q = _q_up(c_q, _pad_w_uq(mla_w_uq[j]), cos_t, sin_t, batch, seq)
            k, v = _kv_up(c_kv, mla_w_ukv[j].astype(BF16), k_rope, batch, seq)
            o = _attention(q, k, v).reshape(m, -1)
            x2 = _mm_residual(o, mla_w_o[j].astype(BF16), x2, mod4, i, 2, seq)
        h = _norm_mod(x2, mod4, i, 3, 4, seq)
        act = _ffn_up(h, ffn_w_up[i].astype(BF16), ffn_conv_w[i], ffn_conv_b[i], seq)
        x2 = _mm_residual(act, ffn_w_down[i].astype(BF16), x2, mod4, i, 5, seq)
    return _final_norm(x2, norm_g).reshape(batch, seq, d)
```

```python
import functools
import math

import jax
import jax.numpy as jnp
from jax import lax
from jax.experimental import pallas as pl
from jax.experimental.pallas import tpu as pltpu

CHUNK = 64
EPS = 1e-6
N_MOD = 6
SGU_BLOCK = 128
SGU_GROUPS = 16
MLA_HEADS = 64
Q_LORA = 1536
KV_LORA = 512
QK_NOPE = 128
QK_ROPE = 64
V_DIM = 128
ROPE_THETA = 10000.0
CONV_W = 3

LANES = 128
SUBLANES = 8
VMEM_LIMIT_BYTES = 56 * 1024 * 1024

F32 = jnp.float32
BF16 = jnp.bfloat16
NEG_BIG = -1e30


def _params(*semantics):
    return pltpu.CompilerParams(dimension_semantics=semantics,
                                vmem_limit_bytes=VMEM_LIMIT_BYTES)


def _tile(dim, want):
    t = min(dim, want)
    while dim % t:
        t -= 1
    return t


def _ada_kernel(c_ref, w_ref, b_ref, o_ref):
    c = c_ref[...]
    c_act = (c / (1.0 + jnp.exp(-c))).astype(BF16)
    w = w_ref[...].astype(BF16)
    o_ref[...] = jnp.dot(c_act, w, preferred_element_type=F32) + b_ref[...]


def _ada_mod(c_pad, ada_w, ada_b):
    depth, d, n = ada_w.shape
    rows = c_pad.shape[0]
    bn = _tile(n, 512)
    return pl.pallas_call(
        _ada_kernel,
        grid=(depth, n // bn),
        in_specs=[
            pl.BlockSpec((rows, d), lambda i, j: (0, 0)),
            pl.BlockSpec((None, d, bn), lambda i, j: (i, 0, j)),
            pl.BlockSpec((None, 1, bn), lambda i, j: (i, 0, j)),
        ],
        out_specs=pl.BlockSpec((None, rows, bn), lambda i, j: (i, 0, j)),
        out_shape=jax.ShapeDtypeStruct((depth, rows, n), F32),
        compiler_params=_params("arbitrary", "arbitrary"),
        name="ada_mod",
    )(c_pad, ada_w, ada_b.reshape(depth, 1, n))


def _normmod_kernel(x_ref, mod_ref, o_ref, *, shift_row, scale_row):
    x = x_ref[...]
    ms = jnp.mean(x * x, axis=-1, keepdims=True)
    y = x * lax.rsqrt(ms + EPS)
    sh = mod_ref[shift_row:shift_row + 1, :]
    sc = mod_ref[scale_row:scale_row + 1, :]
    o_ref[...] = (y * (1.0 + sc) + sh).astype(BF16)


def _norm_mod(x2, mod4, layer, shift_row, scale_row, seq):
    m, d = x2.shape
    ts = _tile(seq, 256)
    per_seq = seq // ts
    return pl.pallas_call(
        functools.partial(_normmod_kernel, shift_row=shift_row, scale_row=scale_row),
        grid=(m // ts,),
        in_specs=[
            pl.BlockSpec((ts, d), lambda i: (i, 0)),
            pl.BlockSpec((None, None, N_MOD, d), lambda i: (layer, i // per_seq, 0, 0)),
        ],
        out_specs=pl.BlockSpec((ts, d), lambda i: (i, 0)),
        out_shape=jax.ShapeDtypeStruct((m, d), BF16),
        compiler_params=_params("arbitrary"),
        name="norm_mod",
    )(x2, mod4)


def _sgu_in_kernel(h_ref, w_ref, b_ref, z_ref):
    a = jnp.dot(h_ref[...], w_ref[...], preferred_element_type=F32) + b_ref[...]
    z = 0.5 * a * (1.0 + lax.erf(a * math.sqrt(0.5)))
    z_ref[...] = z.astype(BF16)


def _sgu_in(h, w, b):
    m, d = h.shape
    n = w.shape[1]
    bm = _tile(m, 1024)
    bn = _tile(n, 1024)
    return pl.pallas_call(
        _sgu_in_kernel,
        grid=(n // bn, m // bm),
        in_specs=[
            pl.BlockSpec((bm, d), lambda j, i: (i, 0)),
            pl.BlockSpec((d, bn), lambda j, i: (0, j)),
            pl.BlockSpec((1, bn), lambda j, i: (0, j)),
        ],
        out_specs=pl.BlockSpec((bm, bn), lambda j, i: (i, j)),
        out_shape=jax.ShapeDtypeStruct((m, n), BF16),
        compiler_params=_params("arbitrary", "arbitrary"),
        name="sgu_in",
    )(h, w, b.reshape(1, n))


def _sgu_gate_kernel(u_ref, v_ref, g_ref, b_ref, ws_ref, bst_ref, o_ref, *, groups):
    blk, hidden = v_ref.shape
    gd = hidden // groups
    s1 = jnp.zeros((blk, 1), F32)
    s2 = jnp.zeros((blk, 1), F32)
    for g in range(groups):
        vg = v_ref[:, g * gd:(g + 1) * gd].astype(F32)
        s1 = s1 + jnp.sum(vg, axis=-1, keepdims=True)
    mu = s1 * (1.0 / hidden)
    for g in range(groups):
        vg = v_ref[:, g * gd:(g + 1) * gd].astype(F32) - mu
        s2 = s2 + jnp.sum(vg * vg, axis=-1, keepdims=True)
    rstd = lax.rsqrt(s2 * (1.0 / hidden) + EPS)
    t_out = lax.broadcasted_iota(jnp.int32, (blk, blk), 0) // CHUNK
    s_in = lax.broadcasted_iota(jnp.int32, (blk, blk), 1) // CHUNK
    causal = s_in <= t_out
    for g in range(groups):
        sl = slice(g * gd, (g + 1) * gd)
        vn = (v_ref[:, sl].astype(F32) - mu) * rstd * g_ref[:, sl] + b_ref[:, sl]
        w = jnp.where(causal, ws_ref[g], 0.0).astype(BF16)
        mixed = jnp.dot(w, vn.astype(BF16), preferred_element_type=F32) + bst_ref[:, g:g + 1]
        o_ref[:, sl] = (u_ref[:, sl].astype(F32) * mixed).astype(BF16)


def _sgu_gate(z, ln_g, ln_b, w_s, b_s):
    m, two_h = z.shape
    hidden = two_h // 2
    groups = w_s.shape[0]
    blk = SGU_BLOCK
    return pl.pallas_call(
        functools.partial(_sgu_gate_kernel, groups=groups),
        grid=(m // blk,),
        in_specs=[
            pl.BlockSpec((blk, hidden), lambda i: (i, 0)),
            pl.BlockSpec((blk, hidden), lambda i: (i, 1)),
            pl.BlockSpec((1, hidden), lambda i: (0, 0)),
            pl.BlockSpec((1, hidden), lambda i: (0, 0)),
            pl.BlockSpec((groups, blk, blk), lambda i: (0, 0, 0)),
            pl.BlockSpec((blk, groups), lambda i: (0, 0)),
        ],
        out_specs=pl.BlockSpec((blk, hidden), lambda i: (i, 0)),
        out_shape=jax.ShapeDtypeStruct((m, hidden), BF16),
        compiler_params=_params("arbitrary"),
        name="sgu_gate",
    )(z, z, ln_g.reshape(1, hidden), ln_b.reshape(1, hidden), w_s, b_s.T)


def _mm_res_kernel(a_ref, w_ref, x_ref, mod_ref, o_ref, *, gate_row, nk):
    k = pl.program_id(1)

    @pl.when(k == 0)
    def _():
        o_ref[...] = jnp.zeros_like(o_ref)

    o_ref[...] += jnp.dot(a_ref[...], w_ref[...], preferred_element_type=F32)

    @pl.when(k == nk - 1)
    def _():
        gate = mod_ref[gate_row:gate_row + 1, :]
        o_ref[...] = x_ref[...] + gate * o_ref[...]


def _mm_residual(a, w, x2, mod4, layer, gate_row, seq):
    m, kdim = a.shape
    d = w.shape[1]
    bm = _tile(seq, 512)
    tk = _tile(kdim, 512)
    nk = kdim // tk
    per_seq = seq // bm
    return pl.pallas_call(
        functools.partial(_mm_res_kernel, gate_row=gate_row, nk=nk),
        grid=(m // bm, nk),
        in_specs=[
            pl.BlockSpec((bm, tk), lambda i, k: (i, k)),
            pl.BlockSpec((tk, d), lambda i, k: (k, 0)),
            pl.BlockSpec((bm, d), lambda i, k: (i, 0)),
            pl.BlockSpec((None, None, N_MOD, d), lambda i, k: (layer, i // per_seq, 0, 0)),
        ],
        out_specs=pl.BlockSpec((bm, d), lambda i, k: (i, 0)),
        out_shape=jax.ShapeDtypeStruct((m, d), F32),
        compiler_params=_params("arbitrary", "arbitrary"),
        name="mm_residual",
    )(a, w, x2, mod4)


def _ffn_up_kernel(h_ref, wg_ref, wu_ref, cwg_ref, cwu_ref, cbg_ref, cbu_ref, o_ref,
                   halo_g, halo_u, *, per_seq):
    i = pl.program_id(1)
    first = (i % per_seq) == 0
    h = h_ref[...]
    bm = h.shape[0]

    @pl.when(first)
    def _():
        halo_g[...] = jnp.zeros_like(halo_g)
        halo_u[...] = jnp.zeros_like(halo_u)

    def conv(a, halo, cw_ref, cb_ref):
        prev = halo[...]
        halo[...] = a[bm - SUBLANES:, :]
        row = lax.broadcasted_iota(jnp.int32, a.shape, 0)
        last1 = prev[SUBLANES - 1:SUBLANES, :]
        last2 = prev[SUBLANES - 2:SUBLANES - 1, :]
        a1 = jnp.where(row == 0, last1, pltpu.roll(a, 1, axis=0))
        a2 = jnp.where(row == 0, last2, jnp.where(row == 1, last1, pltpu.roll(a, 2, axis=0)))
        return cb_ref[...] + cw_ref[0:1, :] * a2 + cw_ref[1:2, :] * a1 + cw_ref[2:3, :] * a

    g = conv(jnp.dot(h, wg_ref[...], preferred_element_type=F32), halo_g, cwg_ref, cbg_ref)
    u = conv(jnp.dot(h, wu_ref[...], preferred_element_type=F32), halo_u, cwu_ref, cbu_ref)
    o_ref[...] = (g / (1.0 + jnp.exp(-g)) * u).astype(BF16)


def _ffn_up(h, w_up, conv_w, conv_b, seq):
    m, d = h.shape
    ff = w_up.shape[1] // 2
    bm = _tile(seq, 1024)
    bn = _tile(ff, 512)
    nf = ff // bn
    per_seq = seq // bm
    cb = conv_b.reshape(1, 2 * ff)
    return pl.pallas_call(
        functools.partial(_ffn_up_kernel, per_seq=per_seq),
        grid=(nf, m // bm),
        in_specs=[
            pl.BlockSpec((bm, d), lambda j, i: (i, 0)),
            pl.BlockSpec((d, bn), lambda j, i: (0, j)),
            pl.BlockSpec((d, bn), lambda j, i: (0, j + nf)),
            pl.BlockSpec((CONV_W, bn), lambda j, i: (0, j)),
            pl.BlockSpec((CONV_W, bn), lambda j, i: (0, j + nf)),
            pl.BlockSpec((1, bn), lambda j, i: (0, j)),
            pl.BlockSpec((1, bn), lambda j, i: (0, j + nf)),
        ],
        out_specs=pl.BlockSpec((bm, bn), lambda j, i: (i, j)),
        out_shape=jax.ShapeDtypeStruct((m, ff), BF16),
        scratch_shapes=[pltpu.VMEM((SUBLANES, bn), F32), pltpu.VMEM((SUBLANES, bn), F32)],
        compiler_params=_params("arbitrary", "arbitrary"),
        name="ffn_up",
    )(h, w_up, w_up, conv_w, conv_w, cb, cb)


def _rope_table_kernel(pos_ref, freq_ref, cos_ref, sin_ref):
    ang = pos_ref[...].astype(F32) * freq_ref[...]
    lane = lax.broadcasted_iota(jnp.int32, ang.shape, 1)
    half = QK_ROPE // 2
    live = lane < QK_ROPE
    cos_ref[...] = jnp.where(live, jnp.cos(ang), 0.0)
    sn = jnp.sin(ang)
    sin_ref[...] = jnp.where(live, jnp.where(lane < half, -sn, sn), 0.0)


def _rope_tables(pos_col, freq_row):
    m = pos_col.shape[0]
    bm = _tile(m, 1024)
    return pl.pallas_call(
        _rope_table_kernel,
        grid=(m // bm,),
        in_specs=[pl.BlockSpec((bm, 1), lambda i: (i, 0)),
                  pl.BlockSpec((1, LANES), lambda i: (0, 0))],
        out_specs=[pl.BlockSpec((bm, LANES), lambda i: (i, 0)),
                   pl.BlockSpec((bm, LANES), lambda i: (i, 0))],
        out_shape=[jax.ShapeDtypeStruct((m, LANES), F32)] * 2,
        compiler_params=_params("arbitrary"),
        name="rope_tables",
    )(pos_col, freq_row)


def _rope_128(x, cos_t, sin_t):
    half = QK_ROPE // 2
    lane = lax.broadcasted_iota(jnp.int32, x.shape, 1)
    swapped = jnp.where(lane < half,
                        pltpu.roll(x, LANES - half, axis=1),
                        pltpu.roll(x, half, axis=1))
    return x * cos_t + swapped * sin_t


def _with_chunk_code(block, frame0, is_key):
    lane = lax.broadcasted_iota(jnp.int32, block.shape, 1) - QK_ROPE
    chunk = (frame0 + lax.broadcasted_iota(jnp.int32, block.shape, 0)) // CHUNK
    if is_key:
        code = jnp.where(lane == chunk, NEG_BIG, 0.0)
    else:
        code = jnp.where(lane > chunk, 1.0, 0.0)
    return jnp.where(lane >= 0, code, block)


def _mla_in_kernel(h_ref, w_ref, gq_ref, gkv_ref, cos_ref, sin_ref,
                   cq_ref, ckv_ref, kr_ref, acc_ref, *, nk, per_seq):
    k = pl.program_id(1)
    frame0 = (pl.program_id(0) % per_seq) * h_ref.shape[0]

    @pl.when(k == 0)
    def _():
        acc_ref[...] = jnp.zeros_like(acc_ref)

    acc_ref[...] += jnp.dot(h_ref[...], w_ref[...], preferred_element_type=F32)

    @pl.when(k == nk - 1)
    def _():
        def rms(v, g):
            return v * lax.rsqrt(jnp.mean(v * v, axis=-1, keepdims=True) + EPS) * g

        cq_ref[...] = rms(acc_ref[:, :Q_LORA], gq_ref[...]).astype(BF16)
        ckv_ref[...] = rms(acc_ref[:, Q_LORA:Q_LORA + KV_LORA], gkv_ref[...]).astype(BF16)
        kr = _rope_128(acc_ref[:, Q_LORA + KV_LORA:], cos_ref[...], sin_ref[...])
        kr_ref[...] = _with_chunk_code(kr, frame0, True).astype(BF16)


def _mla_in(h, w_pad, g_q, g_kv, cos_t, sin_t, seq):
    m, d = h.shape
    n = w_pad.shape[1]
    bm = _tile(seq, 512)
    tk = _tile(d, 1024)
    nk = d // tk
    return pl.pallas_call(
        functools.partial(_mla_in_kernel, nk=nk, per_seq=seq // bm),
        grid=(m // bm, nk),
        in_specs=[
            pl.BlockSpec((bm, tk), lambda i, k: (i, k)),
            pl.BlockSpec((tk, n), lambda i, k: (k, 0)),
            pl.BlockSpec((1, Q_LORA), lambda i, k: (0, 0)),
            pl.BlockSpec((1, KV_LORA), lambda i, k: (0, 0)),
            pl.BlockSpec((bm, LANES), lambda i, k: (i, 0)),
            pl.BlockSpec((bm, LANES), lambda i, k: (i, 0)),
        ],
        out_specs=[
            pl.BlockSpec((bm, Q_LORA), lambda i, k: (i, 0)),
            pl.BlockSpec((bm, KV_LORA), lambda i, k: (i, 0)),
            pl.BlockSpec((bm, LANES), lambda i, k: (i, 0)),
        ],
        out_shape=[
            jax.ShapeDtypeStruct((m, Q_LORA), BF16),
            jax.ShapeDtypeStruct((m, KV_LORA), BF16),
            jax.ShapeDtypeStruct((m, LANES), BF16),
        ],
        scratch_shapes=[pltpu.VMEM((bm, n), F32)],
        compiler_params=_params("arbitrary", "arbitrary"),
        name="mla_in",
    )(h, w_pad, g_q.reshape(1, Q_LORA), g_kv.reshape(1, KV_LORA), cos_t, sin_t)


def _q_up_kernel(cq_ref, w_ref, cos_ref, sin_ref, q_ref, *, hb, scale, per_seq):
    acc = jnp.dot(cq_ref[...], w_ref[...], preferred_element_type=F32)
    width = QK_NOPE + LANES
    frame0 = (pl.program_id(1) % per_seq) * cq_ref.shape[0]
    cos_t = cos_ref[...]
    sin_t = sin_ref[...]
    for j in range(hb):
        nope = acc[:, j * width:j * width + QK_NOPE]
        rope = _rope_128(acc[:, j * width + QK_NOPE:(j + 1) * width], cos_t, sin_t)
        q_ref[j, :, :QK_NOPE] = (nope * scale).astype(BF16)
        q_ref[j, :, QK_NOPE:] = _with_chunk_code(rope * scale, frame0, False).astype(BF16)


def _q_up(c_q, w_q_pad, cos_t, sin_t, batch, seq):
    m, r = c_q.shape
    width = QK_NOPE + LANES
    heads = w_q_pad.shape[1] // width
    hb = _tile(heads, 4)
    bm = _tile(seq, 1024)
    per_seq = seq // bm
    scale = (QK_NOPE + QK_ROPE) ** -0.5 * math.log2(math.e)
    return pl.pallas_call(
        functools.partial(_q_up_kernel, hb=hb, scale=scale, per_seq=per_seq),
        grid=(heads // hb, m // bm),
        in_specs=[
            pl.BlockSpec((bm, r), lambda j, i: (i, 0)),
            pl.BlockSpec((r, hb * width), lambda j, i: (0, j)),
            pl.BlockSpec((bm, LANES), lambda j, i: (i, 0)),
            pl.BlockSpec((bm, LANES), lambda j, i: (i, 0)),
        ],
        out_specs=pl.BlockSpec((None, hb, bm, width),
                               lambda j, i: (i // per_seq, j, i % per_seq, 0)),
        out_shape=jax.ShapeDtypeStruct((batch, heads, seq, width), BF16),
        compiler_params=_params("arbitrary", "arbitrary"),
        name="q_up",
    )(c_q, w_q_pad, cos_t, sin_t)


def _kv_up_kernel(ckv_ref, w_ref, kr_ref, k_ref, v_ref, *, hb):
    acc = jnp.dot(ckv_ref[...], w_ref[...], preferred_element_type=F32)
    width = QK_NOPE + V_DIM
    kr = kr_ref[...]
    ones = jnp.ones((acc.shape[0], LANES), BF16)
    for j in range(hb):
        k_ref[j, :, :QK_NOPE] = acc[:, j * width:j * width + QK_NOPE].astype(BF16)
        k_ref[j, :, QK_NOPE:] = kr
        v_ref[j, :, :V_DIM] = acc[:, j * width + QK_NOPE:(j + 1) * width].astype(BF16)
        v_ref[j, :, V_DIM:] = ones


def _kv_up(c_kv, w_ukv, k_rope, batch, seq):
    m, r = c_kv.shape
    width = QK_NOPE + V_DIM
    heads = w_ukv.shape[1] // width
    hb = _tile(heads, 4)
    bm = _tile(seq, 1024)
    per_seq = seq // bm
    return pl.pallas_call(
        functools.partial(_kv_up_kernel, hb=hb),
        grid=(heads // hb, m // bm),
        in_specs=[
            pl.BlockSpec((bm, r), lambda j, i: (i, 0)),
            pl.BlockSpec((r, hb * width), lambda j, i: (0, j)),
            pl.BlockSpec((bm, LANES), lambda j, i: (i, 0)),
        ],
        out_specs=[
            pl.BlockSpec((None, hb, bm, QK_NOPE + LANES),
                         lambda j, i: (i // per_seq, j, i % per_seq, 0)),
            pl.BlockSpec((None, hb, bm, V_DIM + LANES),
                         lambda j, i: (i // per_seq, j, i % per_seq, 0)),
        ],
        out_shape=[
            jax.ShapeDtypeStruct((batch, heads, seq, QK_NOPE + LANES), BF16),
            jax.ShapeDtypeStruct((batch, heads, seq, V_DIM + LANES), BF16),
        ],
        compiler_params=_params("arbitrary", "arbitrary"),
        name="kv_up",
    )(c_kv, w_ukv, k_rope)


def _attn_kernel(q_ref, k_ref, v_ref, o_ref, m_sc, acc_sc, *, hb, tq):
    qi = pl.program_id(2)
    nt_dims = (((1,), (1,)), ((), ()))
    m_sc[...] = jnp.full(m_sc.shape, NEG_BIG, F32)
    acc_sc[...] = jnp.zeros_like(acc_sc)
    qs = [q_ref[j] for j in range(hb)]

    def k_tile(t, carry):
        k0 = pl.multiple_of(t * tq, tq)
        ss = [lax.dot_general(qs[j], k_ref[j, pl.ds(k0, tq), :], nt_dims,
                              preferred_element_type=F32) for j in range(hb)]
        ps = []
        for j in range(hb):
            m_old = m_sc[j]
            m_new = jnp.maximum(m_old, jnp.max(ss[j], axis=-1, keepdims=True))
            m_sc[j] = m_new
            p = jnp.exp2(ss[j] - pltpu.repeat(m_new, tq // LANES, axis=1))
            ps.append((jnp.exp2(m_old - m_new), p.astype(BF16)))
        for j in range(hb):
            alpha, p = ps[j]
            pv = jnp.dot(p, v_ref[j, pl.ds(k0, tq), :], preferred_element_type=F32)
            acc_sc[j] = pltpu.repeat(alpha, (V_DIM + LANES) // LANES, axis=1) * acc_sc[j] + pv
        return carry

    lax.fori_loop(0, qi + 1, k_tile, 0)
    for j in range(hb):
        acc = acc_sc[j]
        o_ref[:, j * V_DIM:(j + 1) * V_DIM] = (acc[:, :V_DIM] / acc[:, V_DIM:]).astype(BF16)


def _attention(q, k, v):
    batch, heads, seq, width = q.shape
    assert seq // CHUNK <= LANES - QK_ROPE, "chunk code needs one spare lane per chunk"
    hb = _tile(heads, 4)
    tq = _tile(seq, 512)
    return pl.pallas_call(
        functools.partial(_attn_kernel, hb=hb, tq=tq),
        grid=(batch, heads // hb, seq // tq),
        in_specs=[
            pl.BlockSpec((None, hb, tq, width), lambda b, h, i: (b, h, i, 0)),
            pl.BlockSpec((None, hb, seq, width), lambda b, h, i: (b, h, 0, 0)),
            pl.BlockSpec((None, hb, seq, V_DIM + LANES), lambda b, h, i: (b, h, 0, 0)),
        ],
        out_specs=pl.BlockSpec((None, tq, hb * V_DIM), lambda b, h, i: (b, i, h)),
        out_shape=jax.ShapeDtypeStruct((batch, seq, heads * V_DIM), BF16),
        scratch_shapes=[pltpu.VMEM((hb, tq, LANES), F32),
                        pltpu.VMEM((hb, tq, V_DIM + LANES), F32)],
        compiler_params=_params("arbitrary", "arbitrary", "arbitrary"),
        name="attention",
    )(q, k, v)


def _final_norm_kernel(x_ref, g_ref, o_ref):
    x = x_ref[...]
    o_ref[...] = x * lax.rsqrt(jnp.mean(x * x, axis=-1, keepdims=True) + EPS) * g_ref[...]


def _final_norm(x2, g):
    m, d = x2.shape
    ts = _tile(m, 256)
    return pl.pallas_call(
        _final_norm_kernel,
        grid=(m // ts,),
        in_specs=[pl.BlockSpec((ts, d), lambda i: (i, 0)),
                  pl.BlockSpec((1, d), lambda i: (0, 0))],
        out_specs=pl.BlockSpec((ts, d), lambda i: (i, 0)),
        out_shape=jax.ShapeDtypeStruct((m, d), F32),
        compiler_params=_params("arbitrary"),
        name="final_norm",
    )(x2, g.reshape(1, d))


def _pad_mla_w_in(w):
    return jnp.pad(w, ((0, 0), (0, LANES - QK_ROPE))).astype(BF16)


def _pad_w_uq(w):
    r = w.shape[0]
    heads = w.shape[1] // (QK_NOPE + QK_ROPE)
    w3 = w.reshape(r, heads, QK_NOPE + QK_ROPE)
    w3 = jnp.pad(w3, ((0, 0), (0, 0), (0, LANES - QK_ROPE)))
    return w3.reshape(r, heads * (QK_NOPE + LANES)).astype(BF16)


def kernel(x, c, positions, ada_w, ada_b, sgu_w_in, sgu_b_in, sgu_ln_g, sgu_ln_b, sgu_w_s, sgu_b_s, sgu_w_out, mla_w_in, mla_g_q, mla_g_kv, mla_w_uq, mla_w_ukv, mla_w_o, ffn_w_up, ffn_conv_w, ffn_conv_b, ffn_w_down, norm_g):
    batch, seq, d = x.shape
    depth = ada_w.shape[0]
    m = batch * seq

    c_rows = 2 * SUBLANES
    c_pad = jnp.pad(c, ((0, c_rows - batch), (0, 0)))
    mod = _ada_mod(c_pad, ada_w, ada_b)[:, :batch]
    mod4 = mod.reshape(depth, batch, N_MOD, d)

    inv_freq = ROPE_THETA ** (-jnp.arange(0, QK_ROPE, 2, dtype=F32) / QK_ROPE)
    freq_row = jnp.concatenate([inv_freq, inv_freq, jnp.zeros((LANES - QK_ROPE,), F32)]).reshape(1, LANES)
    cos_t, sin_t = _rope_tables(positions.reshape(m, 1), freq_row)

    x2 = x.reshape(m, d)
    for i in range(depth):
        j = i // 2
        h = _norm_mod(x2, mod4, i, 0, 1, seq)
        if i % 2 == 0:
            z = _sgu_in(h, sgu_w_in[j].astype(BF16), sgu_b_in[j])
            gated = _sgu_gate(z, sgu_ln_g[j], sgu_ln_b[j], sgu_w_s[j], sgu_b_s[j])
            x2 = _mm_residual(gated, sgu_w_out[j].astype(BF16), x2, mod4, i, 2, seq)
        else:
            c_q, c_kv, k_rope = _mla_in(h, _pad_mla_w_in(mla_w_in[j]), mla_g_q[j], mla_g_kv[j],
                                        cos_t, sin_t, seq)
            q = _q_up(c_q, _pad_w_uq(mla_w_uq[j]), cos_t, sin_t, batch, seq)
            k, v = _kv_up(c_kv, mla_w_ukv[j].astype(BF16), k_rope, batch, seq)
            o = _attention(q, k, v).reshape(m, -1)
            x2 = _mm_residual(o, mla_w_o[j].astype(BF16), x2, mod4, i, 2, seq)
        h = _norm_mod(x2, mod4, i, 3, 4, seq)
        act = _ffn_up(h, ffn_w_up[i].astype(BF16), ffn_conv_w[i], ffn_conv_b[i], seq)
        x2 = _mm_residual(act, ffn_w_down[i].astype(BF16), x2, mod4, i, 5, seq)
    return _final_norm(x2, norm_g).reshape(batch, seq, d)
```

```python
import functools
import math

import jax
import jax.numpy as jnp
from jax import lax
from jax.experimental import pallas as pl
from jax.experimental.pallas import tpu as pltpu

CHUNK = 64
EPS = 1e-6
N_MOD = 6
SGU_BLOCK = 128
SGU_GROUPS = 16
MLA_HEADS = 64
Q_LORA = 1536
KV_LORA = 512
QK_NOPE = 128
QK_ROPE = 64
V_DIM = 128
ROPE_THETA = 10000.0
CONV_W = 3

LANES = 128
SUBLANES = 8
VMEM_LIMIT_BYTES = 56 * 1024 * 1024

F32 = jnp.float32
BF16 = jnp.bfloat16
NEG_BIG = -1e30


def _params(*semantics):
    return pltpu.CompilerParams(dimension_semantics=semantics,
                                vmem_limit_bytes=VMEM_LIMIT_BYTES)


def _tile(dim, want):
    t = min(dim, want)
    while dim % t:
        t -= 1
    return t


def _ada_kernel(c_ref, w_ref, b_ref, o_ref):
    c = c_ref[...]
    c_act = (c / (1.0 + jnp.exp(-c))).astype(BF16)
    w = w_ref[...].astype(BF16)
    o_ref[...] = jnp.dot(c_act, w, preferred_element_type=F32) + b_ref[...]


def _ada_mod(c_pad, ada_w, ada_b):
    depth, d, n = ada_w.shape
    rows = c_pad.shape[0]
    bn = _tile(n, 512)
    return pl.pallas_call(
        _ada_kernel,
        grid=(depth, n // bn),
        in_specs=[
            pl.BlockSpec((rows, d), lambda i, j: (0, 0)),
            pl.BlockSpec((None, d, bn), lambda i, j: (i, 0, j)),
            pl.BlockSpec((None, 1, bn), lambda i, j: (i, 0, j)),
        ],
        out_specs=pl.BlockSpec((None, rows, bn), lambda i, j: (i, 0, j)),
        out_shape=jax.ShapeDtypeStruct((depth, rows, n), F32),
        compiler_params=_params("arbitrary", "arbitrary"),
        name="ada_mod",
    )(c_pad, ada_w, ada_b.reshape(depth, 1, n))


def _normmod_kernel(x_ref, mod_ref, o_ref, *, shift_row, scale_row):
    x = x_ref[...]
    ms = jnp.mean(x * x, axis=-1, keepdims=True)
    y = x * lax.rsqrt(ms + EPS)
    sh = mod_ref[shift_row:shift_row + 1, :]
    sc = mod_ref[scale_row:scale_row + 1, :]
    o_ref[...] = (y * (1.0 + sc) + sh).astype(BF16)


def _norm_mod(x2, mod4, layer, shift_row, scale_row, seq):
    m, d = x2.shape
    ts = _tile(seq, 256)
    per_seq = seq // ts
    return pl.pallas_call(
        functools.partial(_normmod_kernel, shift_row=shift_row, scale_row=scale_row),
        grid=(m // ts,),
        in_specs=[
            pl.BlockSpec((ts, d), lambda i: (i, 0)),
            pl.BlockSpec((None, None, N_MOD, d), lambda i: (layer, i // per_seq, 0, 0)),
        ],
        out_specs=pl.BlockSpec((ts, d), lambda i: (i, 0)),
        out_shape=jax.ShapeDtypeStruct((m, d), BF16),
        compiler_params=_params("arbitrary"),
        name="norm_mod",
    )(x2, mod4)


def _sgu_in_kernel(h_ref, w_ref, b_ref, z_ref, w_bf):
    @pl.when(pl.program_id(1) == 0)
    def _():
        w_bf[...] = w_ref[...].astype(BF16)

    a = jnp.dot(h_ref[...], w_bf[...], preferred_element_type=F32) + b_ref[...]
    z = 0.5 * a * (1.0 + lax.erf(a * math.sqrt(0.5)))
    z_ref[...] = z.astype(BF16)


def _sgu_in(h, w_all, layer, b):
    m, d = h.shape
    n = w_all.shape[2]
    bm = _tile(m, 1024)
    bn = _tile(n, 512)
    return pl.pallas_call(
        _sgu_in_kernel,
        grid=(n // bn, m // bm),
        in_specs=[
            pl.BlockSpec((bm, d), lambda j, i: (i, 0)),
            pl.BlockSpec((None, d, bn), lambda j, i: (layer, 0, j)),
            pl.BlockSpec((1, bn), lambda j, i: (0, j)),
        ],
        out_specs=pl.BlockSpec((bm, bn), lambda j, i: (i, j)),
        out_shape=jax.ShapeDtypeStruct((m, n), BF16),
        scratch_shapes=[pltpu.VMEM((d, bn), BF16)],
        compiler_params=_params("arbitrary", "arbitrary"),
        name="sgu_in",
    )(h, w_all, b.reshape(1, n))


def _sgu_gate_kernel(u_ref, v_ref, g_ref, b_ref, ws_ref, bst_ref, o_ref, *, groups):
    blk, hidden = v_ref.shape
    gd = hidden // groups
    s1 = jnp.zeros((blk, 1), F32)
    s2 = jnp.zeros((blk, 1), F32)
    for g in range(groups):
        vg = v_ref[:, g * gd:(g + 1) * gd].astype(F32)
        s1 = s1 + jnp.sum(vg, axis=-1, keepdims=True)
    mu = s1 * (1.0 / hidden)
    for g in range(groups):
        vg = v_ref[:, g * gd:(g + 1) * gd].astype(F32) - mu
        s2 = s2 + jnp.sum(vg * vg, axis=-1, keepdims=True)
    rstd = lax.rsqrt(s2 * (1.0 / hidden) + EPS)
    t_out = lax.broadcasted_iota(jnp.int32, (blk, blk), 0) // CHUNK
    s_in = lax.broadcasted_iota(jnp.int32, (blk, blk), 1) // CHUNK
    causal = s_in <= t_out
    for g in range(groups):
        sl = slice(g * gd, (g + 1) * gd)
        vn = (v_ref[:, sl].astype(F32) - mu) * rstd * g_ref[:, sl] + b_ref[:, sl]
        w = jnp.where(causal, ws_ref[g], 0.0).astype(BF16)
        mixed = jnp.dot(w, vn.astype(BF16), preferred_element_type=F32) + bst_ref[:, g:g + 1]
        o_ref[:, sl] = (u_ref[:, sl].astype(F32) * mixed).astype(BF16)


def _sgu_gate(z, ln_g, ln_b, w_s, b_s):
    m, two_h = z.shape
    hidden = two_h // 2
    groups = w_s.shape[0]
    blk = SGU_BLOCK
    return pl.pallas_call(
        functools.partial(_sgu_gate_kernel, groups=groups),
        grid=(m // blk,),
        in_specs=[
            pl.BlockSpec((blk, hidden), lambda i: (i, 0)),
            pl.BlockSpec((blk, hidden), lambda i: (i, 1)),
            pl.BlockSpec((1, hidden), lambda i: (0, 0)),
            pl.BlockSpec((1, hidden), lambda i: (0, 0)),
            pl.BlockSpec((groups, blk, blk), lambda i: (0, 0, 0)),
            pl.BlockSpec((blk, groups), lambda i: (0, 0)),
        ],
        out_specs=pl.BlockSpec((blk, hidden), lambda i: (i, 0)),
        out_shape=jax.ShapeDtypeStruct((m, hidden), BF16),
        compiler_params=_params("arbitrary"),
        name="sgu_gate",
    )(z, z, ln_g.reshape(1, hidden), ln_b.reshape(1, hidden), w_s, b_s.T)


EPILOGUE_ROWS = 128


def _mm_res_kernel(a_ref, w_ref, x_ref, mod_ref, nxt_ref, o_ref, *h_ref, gate_row, nk, next_rows):
    k = pl.program_id(1)

    @pl.when(k == 0)
    def _():
        o_ref[...] = jnp.zeros_like(o_ref)

    o_ref[...] += jnp.dot(a_ref[...], w_ref[...], preferred_element_type=F32)

    @pl.when(k == nk - 1)
    def _():
        gate = mod_ref[gate_row:gate_row + 1, :]
        for r in range(0, o_ref.shape[0], EPILOGUE_ROWS):
            rows = slice(r, r + EPILOGUE_ROWS)
            x_new = x_ref[rows, :] + gate * o_ref[rows, :]
            y = x_new * lax.rsqrt(jnp.mean(x_new * x_new, axis=-1, keepdims=True) + EPS)
            if next_rows is None:
                o_ref[rows, :] = y * nxt_ref[...]
            else:
                shift_row, scale_row = next_rows
                o_ref[rows, :] = x_new
                h_ref[0][rows, :] = (y * (1.0 + nxt_ref[scale_row:scale_row + 1, :])
                                     + nxt_ref[shift_row:shift_row + 1, :]).astype(BF16)


def _mm_residual(a, w, x2, mod4, layer, gate_row, seq, next_norm, norm_g=None):
    m, kdim = a.shape
    d = w.shape[1]
    bm = _tile(seq, 512)
    tk = _tile(kdim, 512)
    nk = kdim // tk
    per_seq = seq // bm
    mod_spec = lambda lyr: pl.BlockSpec((None, None, N_MOD, d), lambda i, k: (lyr, i // per_seq, 0, 0))
    row_spec = pl.BlockSpec((bm, d), lambda i, k: (i, 0))
    if next_norm is None:
        nxt, nxt_spec, next_rows = norm_g.reshape(1, d), pl.BlockSpec((1, d), lambda i, k: (0, 0)), None
        out_specs, out_shape = row_spec, jax.ShapeDtypeStruct((m, d), F32)
    else:
        nxt, nxt_spec, next_rows = mod4, mod_spec(next_norm[0]), next_norm[1:]
        out_specs = [row_spec, row_spec]
        out_shape = [jax.ShapeDtypeStruct((m, d), F32), jax.ShapeDtypeStruct((m, d), BF16)]
    return pl.pallas_call(
        functools.partial(_mm_res_kernel, gate_row=gate_row, nk=nk, next_rows=next_rows),
        grid=(m // bm, nk),
        in_specs=[
            pl.BlockSpec((bm, tk), lambda i, k: (i, k)),
            pl.BlockSpec((tk, d), lambda i, k: (k, 0)),
            pl.BlockSpec((bm, d), lambda i, k: (i, 0), pipeline_mode=pl.Buffered(1)),
            mod_spec(layer),
            nxt_spec,
        ],
        out_specs=out_specs,
        out_shape=out_shape,
        compiler_params=_params("arbitrary", "arbitrary"),
        name="mm_residual",
    )(a, w, x2, mod4, nxt)


FFN_ROW_SPLIT = 8


def _ffn_up_kernel(h_ref, wg_ref, wu_ref, cwg_ref, cwu_ref, cbg_ref, cbu_ref, o_ref,
                   w_bf, a_sc, *, per_seq):
    i = pl.program_id(1)
    bm = h_ref.shape[0]
    bn = wg_ref.shape[1]
    sub = bm // FFN_ROW_SPLIT
    head = slice(0, SUBLANES)
    tail = slice(sub, sub + SUBLANES)

    @pl.when(i == 0)
    def _():
        w_bf[:, :bn] = wg_ref[...].astype(BF16)
        w_bf[:, bn:] = wu_ref[...].astype(BF16)

    @pl.when(i % per_seq == 0)
    def _():
        a_sc[0, head, :] = jnp.zeros((SUBLANES, 2 * bn), F32)

    cw = jnp.concatenate([cwg_ref[...], cwu_ref[...]], axis=1)
    cb = jnp.concatenate([cbg_ref[...], cbu_ref[...]], axis=1)

    def matmul(r):
        a_sc[r % 2, SUBLANES:, :] = jnp.dot(h_ref[r * sub:(r + 1) * sub, :], w_bf[...],
                                            preferred_element_type=F32)

    matmul(0)
    for r in range(FFN_ROW_SPLIT):
        slot = r % 2
        a_sc[(r + 1) % 2 if r + 1 < FFN_ROW_SPLIT else 0, head, :] = a_sc[slot, tail, :]
        if r + 1 < FFN_ROW_SPLIT:
            matmul(r + 1)
        c = (cb + cw[0:1, :] * a_sc[slot, SUBLANES - 2:SUBLANES - 2 + sub, :]
             + cw[1:2, :] * a_sc[slot, SUBLANES - 1:SUBLANES - 1 + sub, :]
             + cw[2:3, :] * a_sc[slot, SUBLANES:, :])
        g, u = c[:, :bn], c[:, bn:]
        o_ref[r * sub:(r + 1) * sub, :] = (g / (1.0 + jnp.exp(-g)) * u).astype(BF16)


def _ffn_up(h, w_up_all, layer, conv_w, conv_b, seq):
    m, d = h.shape
    ff = w_up_all.shape[2] // 2
    bm = _tile(seq, 1024)
    bn = _tile(ff, 256)
    nf = ff // bn
    per_seq = seq // bm
    cb = conv_b.reshape(1, 2 * ff)
    w_up = w_up_all
    return pl.pallas_call(
        functools.partial(_ffn_up_kernel, per_seq=per_seq),
        grid=(nf, m // bm),
        in_specs=[
            pl.BlockSpec((bm, d), lambda j, i: (i, 0)),
            pl.BlockSpec((None, d, bn), lambda j, i: (layer, 0, j)),
            pl.BlockSpec((None, d, bn), lambda j, i: (layer, 0, j + nf)),
            pl.BlockSpec((CONV_W, bn), lambda j, i: (0, j)),
            pl.BlockSpec((CONV_W, bn), lambda j, i: (0, j + nf)),
            pl.BlockSpec((1, bn), lambda j, i: (0, j)),
            pl.BlockSpec((1, bn), lambda j, i: (0, j + nf)),
        ],
        out_specs=pl.BlockSpec((bm, bn), lambda j, i: (i, j)),
        out_shape=jax.ShapeDtypeStruct((m, ff), BF16),
        scratch_shapes=[pltpu.VMEM((d, 2 * bn), BF16),
                        pltpu.VMEM((2, bm // FFN_ROW_SPLIT + SUBLANES, 2 * bn), F32)],
        compiler_params=_params("arbitrary", "arbitrary"),
        name="ffn_up",
    )(h, w_up, w_up, conv_w, conv_w, cb, cb)


def _rope_table_kernel(pos_ref, freq_ref, cos_ref, sin_ref):
    ang = pos_ref[...].astype(F32) * freq_ref[...]
    lane = lax.broadcasted_iota(jnp.int32, ang.shape, 1)
    half = QK_ROPE // 2
    live = lane < QK_ROPE
    cos_ref[...] = jnp.where(live, jnp.cos(ang), 0.0)
    sn = jnp.sin(ang)
    sin_ref[...] = jnp.where(live, jnp.where(lane < half, -sn, sn), 0.0)


def _rope_tables(pos_col, freq_row):
    m = pos_col.shape[0]
    bm = _tile(m, 1024)
    return pl.pallas_call(
        _rope_table_kernel,
        grid=(m // bm,),
        in_specs=[pl.BlockSpec((bm, 1), lambda i: (i, 0)),
                  pl.BlockSpec((1, LANES), lambda i: (0, 0))],
        out_specs=[pl.BlockSpec((bm, LANES), lambda i: (i, 0)),
                   pl.BlockSpec((bm, LANES), lambda i: (i, 0))],
        out_shape=[jax.ShapeDtypeStruct((m, LANES), F32)] * 2,
        compiler_params=_params("arbitrary"),
        name="rope_tables",
    )(pos_col, freq_row)


def _rope_128(x, cos_t, sin_t):
    half = QK_ROPE // 2
    lane = lax.broadcasted_iota(jnp.int32, x.shape, 1)
    swapped = jnp.where(lane < half,
                        pltpu.roll(x, LANES - half, axis=1),
                        pltpu.roll(x, half, axis=1))
    return x * cos_t + swapped * sin_t


def _with_chunk_code(block, frame0, is_key):
    lane = lax.broadcasted_iota(jnp.int32, block.shape, 1) - QK_ROPE
    chunk = (frame0 + lax.broadcasted_iota(jnp.int32, block.shape, 0)) // CHUNK
    if is_key:
        code = jnp.where(lane == chunk, NEG_BIG, 0.0)
    else:
        code = jnp.where(lane > chunk, 1.0, 0.0)
    return jnp.where(lane >= 0, code, block)


def _mla_in_kernel(h_ref, w_ref, gq_ref, gkv_ref, cos_ref, sin_ref,
                   cq_ref, ckv_ref, kr_ref, acc_ref, *, nk, per_seq):
    k = pl.program_id(1)
    frame0 = (pl.program_id(0) % per_seq) * h_ref.shape[0]

    @pl.when(k == 0)
    def _():
        acc_ref[...] = jnp.zeros_like(acc_ref)

    acc_ref[...] += jnp.dot(h_ref[...], w_ref[...], preferred_element_type=F32)

    @pl.when(k == nk - 1)
    def _():
        def rms(v, g):
            return v * lax.rsqrt(jnp.mean(v * v, axis=-1, keepdims=True) + EPS) * g

        cq_ref[...] = rms(acc_ref[:, :Q_LORA], gq_ref[...]).astype(BF16)
        ckv_ref[...] = rms(acc_ref[:, Q_LORA:Q_LORA + KV_LORA], gkv_ref[...]).astype(BF16)
        kr = _rope_128(acc_ref[:, Q_LORA + KV_LORA:], cos_ref[...], sin_ref[...])
        kr_ref[...] = _with_chunk_code(kr, frame0, True).astype(BF16)


def _mla_in(h, w_pad, g_q, g_kv, cos_t, sin_t, seq):
    m, d = h.shape
    n = w_pad.shape[1]
    bm = _tile(seq, 512)
    tk = _tile(d, 1024)
    nk = d // tk
    return pl.pallas_call(
        functools.partial(_mla_in_kernel, nk=nk, per_seq=seq // bm),
        grid=(m // bm, nk),
        in_specs=[
            pl.BlockSpec((bm, tk), lambda i, k: (i, k)),
            pl.BlockSpec((tk, n), lambda i, k: (k, 0)),
            pl.BlockSpec((1, Q_LORA), lambda i, k: (0, 0)),
            pl.BlockSpec((1, KV_LORA), lambda i, k: (0, 0)),
            pl.BlockSpec((bm, LANES), lambda i, k: (i, 0)),
            pl.BlockSpec((bm, LANES), lambda i, k: (i, 0)),
        ],
        out_specs=[
            pl.BlockSpec((bm, Q_LORA), lambda i, k: (i, 0)),
            pl.BlockSpec((bm, KV_LORA), lambda i, k: (i, 0)),
            pl.BlockSpec((bm, LANES), lambda i, k: (i, 0)),
        ],
        out_shape=[
            jax.ShapeDtypeStruct((m, Q_LORA), BF16),
            jax.ShapeDtypeStruct((m, KV_LORA), BF16),
            jax.ShapeDtypeStruct((m, LANES), BF16),
        ],
        scratch_shapes=[pltpu.VMEM((bm, n), F32)],
        compiler_params=_params("arbitrary", "arbitrary"),
        name="mla_in",
    )(h, w_pad, g_q.reshape(1, Q_LORA), g_kv.reshape(1, KV_LORA), cos_t, sin_t)


def _q_up_kernel(cq_ref, w_ref, cos_ref, sin_ref, q_ref, *, hb, scale, per_seq):
    acc = jnp.dot(cq_ref[...], w_ref[...], preferred_element_type=F32)
    width = QK_NOPE + LANES
    frame0 = (pl.program_id(1) % per_seq) * cq_ref.shape[0]
    cos_t = cos_ref[...]
    sin_t = sin_ref[...]
    for j in range(hb):
        nope = acc[:, j * width:j * width + QK_NOPE]
        rope = _rope_128(acc[:, j * width + QK_NOPE:(j + 1) * width], cos_t, sin_t)
        q_ref[j, :, :QK_NOPE] = (nope * scale).astype(BF16)
        q_ref[j, :, QK_NOPE:] = _with_chunk_code(rope * scale, frame0, False).astype(BF16)


def _q_up(c_q, w_q_pad, cos_t, sin_t, batch, seq):
    m, r = c_q.shape
    width = QK_NOPE + LANES
    heads = w_q_pad.shape[1] // width
    hb = _tile(heads, 4)
    bm = _tile(seq, 1024)
    per_seq = seq // bm
    scale = (QK_NOPE + QK_ROPE) ** -0.5 * math.log2(math.e)
    return pl.pallas_call(
        functools.partial(_q_up_kernel, hb=hb, scale=scale, per_seq=per_seq),
        grid=(heads // hb, m // bm),
        in_specs=[
            pl.BlockSpec((bm, r), lambda j, i: (i, 0)),
            pl.BlockSpec((r, hb * width), lambda j, i: (0, j)),
            pl.BlockSpec((bm, LANES), lambda j, i: (i, 0)),
            pl.BlockSpec((bm, LANES), lambda j, i: (i, 0)),
        ],
        out_specs=pl.BlockSpec((None, hb, bm, width),
                               lambda j, i: (i // per_seq, j, i % per_seq, 0)),
        out_shape=jax.ShapeDtypeStruct((batch, heads, seq, width), BF16),
        compiler_params=_params("arbitrary", "arbitrary"),
        name="q_up",
    )(c_q, w_q_pad, cos_t, sin_t)


def _kv_up_kernel(ckv_ref, w_ref, kr_ref, k_ref, v_ref, *, hb):
    acc = jnp.dot(ckv_ref[...], w_ref[...], preferred_element_type=F32)
    width = QK_NOPE + V_DIM
    kr = kr_ref[...]
    ones = jnp.ones((acc.shape[0], LANES), BF16)
    for j in range(hb):
        k_ref[j, :, :QK_NOPE] = acc[:, j * width:j * width + QK_NOPE].astype(BF16)
        k_ref[j, :, QK_NOPE:] = kr
        v_ref[j, :, :V_DIM] = acc[:, j * width + QK_NOPE:(j + 1) * width].astype(BF16)
        v_ref[j, :, V_DIM:] = ones


def _kv_up(c_kv, w_ukv, k_rope, batch, seq):
    m, r = c_kv.shape
    width = QK_NOPE + V_DIM
    heads = w_ukv.shape[1] // width
    hb = _tile(heads, 4)
    bm = _tile(seq, 1024)
    per_seq = seq // bm
    return pl.pallas_call(
        functools.partial(_kv_up_kernel, hb=hb),
        grid=(heads // hb, m // bm),
        in_specs=[
            pl.BlockSpec((bm, r), lambda j, i: (i, 0)),
            pl.BlockSpec((r, hb * width), lambda j, i: (0, j)),
            pl.BlockSpec((bm, LANES), lambda j, i: (i, 0)),
        ],
        out_specs=[
            pl.BlockSpec((None, hb, bm, QK_NOPE + LANES),
                         lambda j, i: (i // per_seq, j, i % per_seq, 0)),
            pl.BlockSpec((None, hb, bm, V_DIM + LANES),
                         lambda j, i: (i // per_seq, j, i % per_seq, 0)),
        ],
        out_shape=[
            jax.ShapeDtypeStruct((batch, heads, seq, QK_NOPE + LANES), BF16),
            jax.ShapeDtypeStruct((batch, heads, seq, V_DIM + LANES), BF16),
        ],
        compiler_params=_params("arbitrary", "arbitrary"),
        name="kv_up",
    )(c_kv, w_ukv, k_rope)


def _attn_kernel(q_ref, k_ref, v_ref, o_ref, m_sc, acc_sc, *, hb, tq):
    qi = pl.program_id(2)
    nt_dims = (((1,), (1,)), ((), ()))
    m_sc[...] = jnp.full(m_sc.shape, NEG_BIG, F32)
    acc_sc[...] = jnp.zeros_like(acc_sc)
    qs = [q_ref[j] for j in range(hb)]

    def k_tile(t, carry):
        k0 = pl.multiple_of(t * tq, tq)
        ss = [lax.dot_general(qs[j], k_ref[j, pl.ds(k0, tq), :], nt_dims,
                              preferred_element_type=F32) for j in range(hb)]
        ps = []
        for j in range(hb):
            m_old = m_sc[j]
            m_new = jnp.maximum(m_old, jnp.max(ss[j], axis=-1, keepdims=True))
            m_sc[j] = m_new
            p = jnp.exp2(ss[j] - jnp.concatenate([m_new] * (tq // LANES), axis=1))
            ps.append((jnp.exp2(m_old - m_new), p.astype(BF16)))
        for j in range(hb):
            alpha, p = ps[j]
            pv = jnp.dot(p, v_ref[j, pl.ds(k0, tq), :], preferred_element_type=F32)
            acc_sc[j] = jnp.concatenate([alpha] * ((V_DIM + LANES) // LANES), axis=1) * acc_sc[j] + pv
        return carry

    lax.fori_loop(0, qi + 1, k_tile, 0)
    for j in range(hb):
        acc = acc_sc[j]
        o_ref[:, j * V_DIM:(j + 1) * V_DIM] = (acc[:, :V_DIM] / acc[:, V_DIM:]).astype(BF16)


def _attention(q, k, v):
    batch, heads, seq, width = q.shape
    assert seq // CHUNK <= LANES - QK_ROPE, "chunk code needs one spare lane per chunk"
    hb = _tile(heads, 4)
    tq = _tile(seq, 512)
    return pl.pallas_call(
        functools.partial(_attn_kernel, hb=hb, tq=tq),
        grid=(batch, heads // hb, seq // tq),
        in_specs=[
            pl.BlockSpec((None, hb, tq, width), lambda b, h, i: (b, h, i, 0)),
            pl.BlockSpec((None, hb, seq, width), lambda b, h, i: (b, h, 0, 0)),
            pl.BlockSpec((None, hb, seq, V_DIM + LANES), lambda b, h, i: (b, h, 0, 0)),
        ],
        out_specs=pl.BlockSpec((None, tq, hb * V_DIM), lambda b, h, i: (b, i, h)),
        out_shape=jax.ShapeDtypeStruct((batch, seq, heads * V_DIM), BF16),
        scratch_shapes=[pltpu.VMEM((hb, tq, LANES), F32),
                        pltpu.VMEM((hb, tq, V_DIM + LANES), F32)],
        compiler_params=_params("arbitrary", "arbitrary", "arbitrary"),
        name="attention",
    )(q, k, v)


def _pad_mla_w_in(w):
    return jnp.pad(w, ((0, 0), (0, LANES - QK_ROPE))).astype(BF16)


def _pad_w_uq(w):
    r = w.shape[0]
    heads = w.shape[1] // (QK_NOPE + QK_ROPE)
    w3 = w.reshape(r, heads, QK_NOPE + QK_ROPE)
    w3 = jnp.pad(w3, ((0, 0), (0, 0), (0, LANES - QK_ROPE)))
    return w3.reshape(r, heads * (QK_NOPE + LANES)).astype(BF16)


def kernel(x, c, positions, ada_w, ada_b, sgu_w_in, sgu_b_in, sgu_ln_g, sgu_ln_b, sgu_w_s, sgu_b_s, sgu_w_out, mla_w_in, mla_g_q, mla_g_kv, mla_w_uq, mla_w_ukv, mla_w_o, ffn_w_up, ffn_conv_w, ffn_conv_b, ffn_w_down, norm_g):
    batch, seq, d = x.shape
    depth = ada_w.shape[0]
    m = batch * seq

    c_rows = 2 * SUBLANES
    c_pad = jnp.pad(c, ((0, c_rows - batch), (0, 0)))
    mod = _ada_mod(c_pad, ada_w, ada_b)[:, :batch]
    mod4 = mod.reshape(depth, batch, N_MOD, d)

    inv_freq = ROPE_THETA ** (-jnp.arange(0, QK_ROPE, 2, dtype=F32) / QK_ROPE)
    freq_row = jnp.concatenate([inv_freq, inv_freq, jnp.zeros((LANES - QK_ROPE,), F32)]).reshape(1, LANES)
    cos_t, sin_t = _rope_tables(positions.reshape(m, 1), freq_row)

    x2 = x.reshape(m, d)
    h = _norm_mod(x2, mod4, 0, 0, 1, seq)
    for i in range(depth):
        j = i // 2
        if i % 2 == 0:
            z = _sgu_in(h, sgu_w_in, j, sgu_b_in[j])
            gated = _sgu_gate(z, sgu_ln_g[j], sgu_ln_b[j], sgu_w_s[j], sgu_b_s[j])
            x2, h = _mm_residual(gated, sgu_w_out[j].astype(BF16), x2, mod4, i, 2, seq, (i, 3, 4))
        else:
            c_q, c_kv, k_rope = _mla_in(h, _pad_mla_w_in(mla_w_in[j]), mla_g_q[j], mla_g_kv[j],
                                        cos_t, sin_t, seq)
            q = _q_up(c_q, _pad_w_uq(mla_w_uq[j]), cos_t, sin_t, batch, seq)
            k, v = _kv_up(c_kv, mla_w_ukv[j].astype(BF16), k_rope, batch, seq)
            o = _attention(q, k, v).reshape(m, -1)
            x2, h = _mm_residual(o, mla_w_o[j].astype(BF16), x2, mod4, i, 2, seq, (i, 3, 4))
        act = _ffn_up(h, ffn_w_up, i, ffn_conv_w[i], ffn_conv_b[i], seq)
        w_down = ffn_w_down[i].astype(BF16)
        if i + 1 < depth:
            x2, h = _mm_residual(act, w_down, x2, mod4, i, 5, seq, (i + 1, 0, 1))
        else:
            out = _mm_residual(act, w_down, x2, mod4, i, 5, seq, None, norm_g)
    return out.reshape(batch, seq, d)
```

```python
import functools
import math

import jax
import jax.numpy as jnp
from jax import lax
from jax.experimental import pallas as pl
from jax.experimental.pallas import tpu as pltpu

CHUNK = 64
EPS = 1e-6
N_MOD = 6
SGU_BLOCK = 128
SGU_GROUPS = 16
MLA_HEADS = 64
Q_LORA = 1536
KV_LORA = 512
QK_NOPE = 128
QK_ROPE = 64
V_DIM = 128
ROPE_THETA = 10000.0
CONV_W = 3

LANES = 128
SUBLANES = 8
VMEM_LIMIT_BYTES = 56 * 1024 * 1024

F32 = jnp.float32
BF16 = jnp.bfloat16
NEG_BIG = -1e30


def _params(*semantics):
    return pltpu.CompilerParams(dimension_semantics=semantics,
                                vmem_limit_bytes=VMEM_LIMIT_BYTES)


def _tile(dim, want):
    t = min(dim, want)
    while dim % t:
        t -= 1
    return t


def _ada_kernel(c_ref, w_ref, b_ref, o_ref):
    c = c_ref[...]
    c_act = (c / (1.0 + jnp.exp(-c))).astype(BF16)
    w = w_ref[...].astype(BF16)
    o_ref[...] = jnp.dot(c_act, w, preferred_element_type=F32) + b_ref[...]


def _ada_mod(c_pad, ada_w, ada_b):
    depth, d, n = ada_w.shape
    rows = c_pad.shape[0]
    bn = _tile(n, 512)
    return pl.pallas_call(
        _ada_kernel,
        grid=(depth, n // bn),
        in_specs=[
            pl.BlockSpec((rows, d), lambda i, j: (0, 0)),
            pl.BlockSpec((None, d, bn), lambda i, j: (i, 0, j)),
            pl.BlockSpec((None, 1, bn), lambda i, j: (i, 0, j)),
        ],
        out_specs=pl.BlockSpec((None, rows, bn), lambda i, j: (i, 0, j)),
        out_shape=jax.ShapeDtypeStruct((depth, rows, n), F32),
        compiler_params=_params("arbitrary", "arbitrary"),
        name="ada_mod",
    )(c_pad, ada_w, ada_b.reshape(depth, 1, n))


def _normmod_kernel(x_ref, mod_ref, o_ref, *, shift_row, scale_row):
    x = x_ref[...]
    ms = jnp.mean(x * x, axis=-1, keepdims=True)
    y = x * lax.rsqrt(ms + EPS)
    sh = mod_ref[shift_row:shift_row + 1, :]
    sc = mod_ref[scale_row:scale_row + 1, :]
    o_ref[...] = (y * (1.0 + sc) + sh).astype(BF16)


def _norm_mod(x2, mod4, layer, shift_row, scale_row, seq):
    m, d = x2.shape
    ts = _tile(seq, 256)
    per_seq = seq // ts
    return pl.pallas_call(
        functools.partial(_normmod_kernel, shift_row=shift_row, scale_row=scale_row),
        grid=(m // ts,),
        in_specs=[
            pl.BlockSpec((ts, d), lambda i: (i, 0)),
            pl.BlockSpec((None, None, N_MOD, d), lambda i: (layer, i // per_seq, 0, 0)),
        ],
        out_specs=pl.BlockSpec((ts, d), lambda i: (i, 0)),
        out_shape=jax.ShapeDtypeStruct((m, d), BF16),
        compiler_params=_params("arbitrary"),
        name="norm_mod",
    )(x2, mod4)


def _sgu_in_kernel(h_ref, w_ref, b_ref, z_ref, w_bf, *, n_col):
    p = pl.program_id(0)
    i = pl.program_id(1)
    w_rows = w_ref.shape[0]

    def cast_chunk():
        w_bf[p % 2, pl.ds(pl.multiple_of(i * w_rows, w_rows), w_rows), :] = w_ref[...].astype(BF16)

    def compute():
        a = jnp.dot(h_ref[...], w_bf[(p + 1) % 2], preferred_element_type=F32) + b_ref[...]
        z = 0.5 * a * (1.0 + lax.erf(a * math.sqrt(0.5)))
        z_ref[...] = z.astype(BF16)

    @pl.when(p == 0)
    def _():
        cast_chunk()

    @pl.when(jnp.logical_and(p > 0, p < n_col))
    def _():
        cast_chunk()
        compute()

    @pl.when(p == n_col)
    def _():
        compute()


def _sgu_in(h, w_all, layer, b):
    m, d = h.shape
    n = w_all.shape[2]
    bm = _tile(m, 1024)
    bn = _tile(n, 1024)
    nc = n // bn
    nm = m // bm
    w_rows = d // nm
    col = lambda p: jnp.maximum(p - 1, 0)
    return pl.pallas_call(
        functools.partial(_sgu_in_kernel, n_col=nc),
        grid=(nc + 1, nm),
        in_specs=[
            pl.BlockSpec((bm, d), lambda p, i: (jnp.where(p == 0, 0, i), 0)),
            pl.BlockSpec((None, w_rows, bn), lambda p, i: (layer, i, jnp.minimum(p, nc - 1))),
            pl.BlockSpec((1, bn), lambda p, i: (0, col(p))),
        ],
        out_specs=pl.BlockSpec((bm, bn), lambda p, i: (jnp.where(p == 0, 0, i), col(p))),
        out_shape=jax.ShapeDtypeStruct((m, n), BF16),
        scratch_shapes=[pltpu.VMEM((2, d, bn), BF16)],
        compiler_params=_params("arbitrary", "arbitrary"),
        name="sgu_in",
    )(h, w_all, b.reshape(1, n))


def _sgu_gate_kernel(u_ref, v_ref, g_ref, b_ref, ws_ref, bst_ref, o_ref, *, groups):
    blk, hidden = v_ref.shape
    gd = hidden // groups
    s1 = jnp.zeros((blk, 1), F32)
    s2 = jnp.zeros((blk, 1), F32)
    for g in range(groups):
        vg = v_ref[:, g * gd:(g + 1) * gd].astype(F32)
        s1 = s1 + jnp.sum(vg, axis=-1, keepdims=True)
    mu = s1 * (1.0 / hidden)
    for g in range(groups):
        vg = v_ref[:, g * gd:(g + 1) * gd].astype(F32) - mu
        s2 = s2 + jnp.sum(vg * vg, axis=-1, keepdims=True)
    rstd = lax.rsqrt(s2 * (1.0 / hidden) + EPS)
    t_out = lax.broadcasted_iota(jnp.int32, (blk, blk), 0) // CHUNK
    s_in = lax.broadcasted_iota(jnp.int32, (blk, blk), 1) // CHUNK
    causal = s_in <= t_out
    for g in range(groups):
        sl = slice(g * gd, (g + 1) * gd)
        vn = (v_ref[:, sl].astype(F32) - mu) * rstd * g_ref[:, sl] + b_ref[:, sl]
        w = jnp.where(causal, ws_ref[g], 0.0).astype(BF16)
        mixed = jnp.dot(w, vn.astype(BF16), preferred_element_type=F32) + bst_ref[:, g:g + 1]
        o_ref[:, sl] = (u_ref[:, sl].astype(F32) * mixed).astype(BF16)


def _sgu_gate(z, ln_g, ln_b, w_s, b_s):
    m, two_h = z.shape
    hidden = two_h // 2
    groups = w_s.shape[0]
    blk = SGU_BLOCK
    return pl.pallas_call(
        functools.partial(_sgu_gate_kernel, groups=groups),
        grid=(m // blk,),
        in_specs=[
            pl.BlockSpec((blk, hidden), lambda i: (i, 0)),
            pl.BlockSpec((blk, hidden), lambda i: (i, 1)),
            pl.BlockSpec((1, hidden), lambda i: (0, 0)),
            pl.BlockSpec((1, hidden), lambda i: (0, 0)),
            pl.BlockSpec((groups, blk, blk), lambda i: (0, 0, 0)),
            pl.BlockSpec((blk, groups), lambda i: (0, 0)),
        ],
        out_specs=pl.BlockSpec((blk, hidden), lambda i: (i, 0)),
        out_shape=jax.ShapeDtypeStruct((m, hidden), BF16),
        compiler_params=_params("arbitrary"),
        name="sgu_gate",
    )(z, z, ln_g.reshape(1, hidden), ln_b.reshape(1, hidden), w_s, b_s.T)


def _mm_res_kernel(a_ref, w_ref, x_ref, mod_ref, o_ref, *, gate_row, nk):
    k = pl.program_id(2)

    @pl.when(k == 0)
    def _():
        o_ref[...] = jnp.zeros_like(o_ref)

    o_ref[...] += jnp.dot(a_ref[...], w_ref[...], preferred_element_type=F32)

    @pl.when(k == nk - 1)
    def _():
        o_ref[...] = x_ref[...] + mod_ref[gate_row:gate_row + 1, :] * o_ref[...]


def _mm_residual(a, w, x2, mod4, layer, gate_row, seq):
    m, kdim = a.shape
    d = w.shape[1]
    bm = _tile(seq, 1024)
    bn = _tile(d, 2048)
    tk = _tile(kdim, 512)
    nk = kdim // tk
    per_seq = seq // bm
    return pl.pallas_call(
        functools.partial(_mm_res_kernel, gate_row=gate_row, nk=nk),
        grid=(m // bm, d // bn, nk),
        in_specs=[
            pl.BlockSpec((bm, tk), lambda i, j, k: (i, k)),
            pl.BlockSpec((tk, bn), lambda i, j, k: (k, j)),
            pl.BlockSpec((bm, bn), lambda i, j, k: (i, j)),
            pl.BlockSpec((None, None, N_MOD, bn), lambda i, j, k: (layer, i // per_seq, 0, j)),
        ],
        out_specs=pl.BlockSpec((bm, bn), lambda i, j, k: (i, j)),
        out_shape=jax.ShapeDtypeStruct((m, d), F32),
        compiler_params=_params("arbitrary", "arbitrary", "arbitrary"),
        name="mm_residual",
    )(a, w, x2, mod4)


FFN_ROW_SPLIT = 8


def _ffn_up_kernel(h_ref, wg_ref, wu_ref, cwg_ref, cwu_ref, cbg_ref, cbu_ref, o_ref,
                   w_bf, a_sc, *, per_seq, n_col):
    p = pl.program_id(0)
    i = pl.program_id(1)
    bm = h_ref.shape[0]
    w_rows, bn = wg_ref.shape
    sub = bm // FFN_ROW_SPLIT
    head = slice(0, SUBLANES)
    tail = slice(sub, sub + SUBLANES)

    def cast_chunk():
        chunk = pl.ds(pl.multiple_of(i * w_rows, w_rows), w_rows)
        w_bf[p % 2, chunk, :bn] = wg_ref[...].astype(BF16)
        w_bf[p % 2, chunk, bn:] = wu_ref[...].astype(BF16)

    def compute():
        use = (p + 1) % 2

        @pl.when(i % per_seq == 0)
        def _():
            a_sc[0, head, :] = jnp.zeros((SUBLANES, 2 * bn), F32)

        cw = jnp.concatenate([cwg_ref[...], cwu_ref[...]], axis=1)
        cb = jnp.concatenate([cbg_ref[...], cbu_ref[...]], axis=1)

        def matmul(r):
            a_sc[r % 2, SUBLANES:, :] = jnp.dot(h_ref[r * sub:(r + 1) * sub, :], w_bf[use],
                                                preferred_element_type=F32)

        matmul(0)
        for r in range(FFN_ROW_SPLIT):
            slot = r % 2
            a_sc[(r + 1) % 2 if r + 1 < FFN_ROW_SPLIT else 0, head, :] = a_sc[slot, tail, :]
            if r + 1 < FFN_ROW_SPLIT:
                matmul(r + 1)
            c = (cb + cw[0:1, :] * a_sc[slot, SUBLANES - 2:SUBLANES - 2 + sub, :]
                 + cw[1:2, :] * a_sc[slot, SUBLANES - 1:SUBLANES - 1 + sub, :]
                 + cw[2:3, :] * a_sc[slot, SUBLANES:, :])
            g, u = c[:, :bn], c[:, bn:]
            o_ref[r * sub:(r + 1) * sub, :] = (g / (1.0 + jnp.exp(-g)) * u).astype(BF16)

    @pl.when(p == 0)
    def _():
        cast_chunk()

    @pl.when(jnp.logical_and(p > 0, p < n_col))
    def _():
        cast_chunk()
        compute()

    @pl.when(p == n_col)
    def _():
        compute()


def _ffn_up(h, w_up_all, layer, conv_w, conv_b, seq):
    m, d = h.shape
    ff = w_up_all.shape[2] // 2
    bm = _tile(seq, 1024)
    bn = _tile(ff, 512)
    nf = ff // bn
    nm = m // bm
    per_seq = seq // bm
    w_rows = d // nm
    cb = conv_b.reshape(1, 2 * ff)
    w_up = w_up_all
    w_col = lambda p: jnp.minimum(p, nf - 1)
    col = lambda p: jnp.maximum(p - 1, 0)
    return pl.pallas_call(
        functools.partial(_ffn_up_kernel, per_seq=per_seq, n_col=nf),
        grid=(nf + 1, nm),
        in_specs=[
            pl.BlockSpec((bm, d), lambda p, i: (jnp.where(p == 0, 0, i), 0)),
            pl.BlockSpec((None, w_rows, bn), lambda p, i: (layer, i, w_col(p))),
            pl.BlockSpec((None, w_rows, bn), lambda p, i: (layer, i, w_col(p) + nf)),
            pl.BlockSpec((CONV_W, bn), lambda p, i: (0, col(p))),
            pl.BlockSpec((CONV_W, bn), lambda p, i: (0, col(p) + nf)),
            pl.BlockSpec((1, bn), lambda p, i: (0, col(p))),
            pl.BlockSpec((1, bn), lambda p, i: (0, col(p) + nf)),
        ],
        out_specs=pl.BlockSpec((bm, bn), lambda p, i: (jnp.where(p == 0, 0, i), col(p))),
        out_shape=jax.ShapeDtypeStruct((m, ff), BF16),
        scratch_shapes=[pltpu.VMEM((2, d, 2 * bn), BF16),
                        pltpu.VMEM((2, bm // FFN_ROW_SPLIT + SUBLANES, 2 * bn), F32)],
        compiler_params=_params("arbitrary", "arbitrary"),
        name="ffn_up",
    )(h, w_up, w_up, conv_w, conv_w, cb, cb)


def _rope_table_kernel(pos_ref, freq_ref, cos_ref, sin_ref):
    ang = pos_ref[...].astype(F32) * freq_ref[...]
    lane = lax.broadcasted_iota(jnp.int32, ang.shape, 1)
    half = QK_ROPE // 2
    live = lane < QK_ROPE
    cos_ref[...] = jnp.where(live, jnp.cos(ang), 0.0)
    sn = jnp.sin(ang)
    sin_ref[...] = jnp.where(live, jnp.where(lane < half, -sn, sn), 0.0)


def _rope_tables(pos_col, freq_row):
    m = pos_col.shape[0]
    bm = _tile(m, 1024)
    return pl.pallas_call(
        _rope_table_kernel,
        grid=(m // bm,),
        in_specs=[pl.BlockSpec((bm, 1), lambda i: (i, 0)),
                  pl.BlockSpec((1, LANES), lambda i: (0, 0))],
        out_specs=[pl.BlockSpec((bm, LANES), lambda i: (i, 0)),
                   pl.BlockSpec((bm, LANES), lambda i: (i, 0))],
        out_shape=[jax.ShapeDtypeStruct((m, LANES), F32)] * 2,
        compiler_params=_params("arbitrary"),
        name="rope_tables",
    )(pos_col, freq_row)


def _rope_128(x, cos_t, sin_t):
    half = QK_ROPE // 2
    lane = lax.broadcasted_iota(jnp.int32, x.shape, 1)
    swapped = jnp.where(lane < half,
                        pltpu.roll(x, LANES - half, axis=1),
                        pltpu.roll(x, half, axis=1))
    return x * cos_t + swapped * sin_t


def _with_chunk_code(block, frame0, is_key):
    lane = lax.broadcasted_iota(jnp.int32, block.shape, 1) - QK_ROPE
    chunk = (frame0 + lax.broadcasted_iota(jnp.int32, block.shape, 0)) // CHUNK
    if is_key:
        code = jnp.where(lane == chunk, NEG_BIG, 0.0)
    else:
        code = jnp.where(lane > chunk, 1.0, 0.0)
    return jnp.where(lane >= 0, code, block)


def _mla_in_kernel(h_ref, w_ref, gq_ref, gkv_ref, cos_ref, sin_ref,
                   cq_ref, ckv_ref, kr_ref, acc_ref, *, nk, per_seq):
    k = pl.program_id(1)
    frame0 = (pl.program_id(0) % per_seq) * h_ref.shape[0]

    @pl.when(k == 0)
    def _():
        acc_ref[...] = jnp.zeros_like(acc_ref)

    acc_ref[...] += jnp.dot(h_ref[...], w_ref[...], preferred_element_type=F32)

    @pl.when(k == nk - 1)
    def _():
        def rms(v, g):
            return v * lax.rsqrt(jnp.mean(v * v, axis=-1, keepdims=True) + EPS) * g

        cq_ref[...] = rms(acc_ref[:, :Q_LORA], gq_ref[...]).astype(BF16)
        ckv_ref[...] = rms(acc_ref[:, Q_LORA:Q_LORA + KV_LORA], gkv_ref[...]).astype(BF16)
        kr = _rope_128(acc_ref[:, Q_LORA + KV_LORA:], cos_ref[...], sin_ref[...])
        kr_ref[...] = _with_chunk_code(kr, frame0, True).astype(BF16)


def _mla_in(h, w_pad, g_q, g_kv, cos_t, sin_t, seq):
    m, d = h.shape
    n = w_pad.shape[1]
    bm = _tile(seq, 512)
    tk = _tile(d, 1024)
    nk = d // tk
    return pl.pallas_call(
        functools.partial(_mla_in_kernel, nk=nk, per_seq=seq // bm),
        grid=(m // bm, nk),
        in_specs=[
            pl.BlockSpec((bm, tk), lambda i, k: (i, k)),
            pl.BlockSpec((tk, n), lambda i, k: (k, 0)),
            pl.BlockSpec((1, Q_LORA), lambda i, k: (0, 0)),
            pl.BlockSpec((1, KV_LORA), lambda i, k: (0, 0)),
            pl.BlockSpec((bm, LANES), lambda i, k: (i, 0)),
            pl.BlockSpec((bm, LANES), lambda i, k: (i, 0)),
        ],
        out_specs=[
            pl.BlockSpec((bm, Q_LORA), lambda i, k: (i, 0)),
            pl.BlockSpec((bm, KV_LORA), lambda i, k: (i, 0)),
            pl.BlockSpec((bm, LANES), lambda i, k: (i, 0)),
        ],
        out_shape=[
            jax.ShapeDtypeStruct((m, Q_LORA), BF16),
            jax.ShapeDtypeStruct((m, KV_LORA), BF16),
            jax.ShapeDtypeStruct((m, LANES), BF16),
        ],
        scratch_shapes=[pltpu.VMEM((bm, n), F32)],
        compiler_params=_params("arbitrary", "arbitrary"),
        name="mla_in",
    )(h, w_pad, g_q.reshape(1, Q_LORA), g_kv.reshape(1, KV_LORA), cos_t, sin_t)


def _q_up_kernel(cq_ref, w_ref, cos_ref, sin_ref, q_ref, *, hb, scale, per_seq):
    acc = jnp.dot(cq_ref[...], w_ref[...], preferred_element_type=F32)
    width = QK_NOPE + LANES
    frame0 = (pl.program_id(1) % per_seq) * cq_ref.shape[0]
    cos_t = cos_ref[...]
    sin_t = sin_ref[...]
    for j in range(hb):
        nope = acc[:, j * width:j * width + QK_NOPE]
        rope = _rope_128(acc[:, j * width + QK_NOPE:(j + 1) * width], cos_t, sin_t)
        q_ref[j, :, :QK_NOPE] = (nope * scale).astype(BF16)
        q_ref[j, :, QK_NOPE:] = _with_chunk_code(rope * scale, frame0, False).astype(BF16)


def _q_up(c_q, w_q_pad, cos_t, sin_t, batch, seq):
    m, r = c_q.shape
    width = QK_NOPE + LANES
    heads = w_q_pad.shape[1] // width
    hb = _tile(heads, 4)
    bm = _tile(seq, 1024)
    per_seq = seq // bm
    scale = (QK_NOPE + QK_ROPE) ** -0.5 * math.log2(math.e)
    return pl.pallas_call(
        functools.partial(_q_up_kernel, hb=hb, scale=scale, per_seq=per_seq),
        grid=(heads // hb, m // bm),
        in_specs=[
            pl.BlockSpec((bm, r), lambda j, i: (i, 0)),
            pl.BlockSpec((r, hb * width), lambda j, i: (0, j)),
            pl.BlockSpec((bm, LANES), lambda j, i: (i, 0)),
            pl.BlockSpec((bm, LANES), lambda j, i: (i, 0)),
        ],
        out_specs=pl.BlockSpec((None, hb, bm, width),
                               lambda j, i: (i // per_seq, j, i % per_seq, 0)),
        out_shape=jax.ShapeDtypeStruct((batch, heads, seq, width), BF16),
        compiler_params=_params("arbitrary", "arbitrary"),
        name="q_up",
    )(c_q, w_q_pad, cos_t, sin_t)


def _kv_up_kernel(ckv_ref, w_ref, kr_ref, k_ref, v_ref, *, hb):
    acc = jnp.dot(ckv_ref[...], w_ref[...], preferred_element_type=F32)
    width = QK_NOPE + V_DIM
    kr = kr_ref[...]
    ones = jnp.ones((acc.shape[0], LANES), BF16)
    for j in range(hb):
        k_ref[j, :, :QK_NOPE] = acc[:, j * width:j * width + QK_NOPE].astype(BF16)
        k_ref[j, :, QK_NOPE:] = kr
        v_ref[j, :, :V_DIM] = acc[:, j * width + QK_NOPE:(j + 1) * width].astype(BF16)
        v_ref[j, :, V_DIM:] = ones


def _kv_up(c_kv, w_ukv, k_rope, batch, seq):
    m, r = c_kv.shape
    width = QK_NOPE + V_DIM
    heads = w_ukv.shape[1] // width
    hb = _tile(heads, 4)
    bm = _tile(seq, 1024)
    per_seq = seq // bm
    return pl.pallas_call(
        functools.partial(_kv_up_kernel, hb=hb),
        grid=(heads // hb, m // bm),
        in_specs=[
            pl.BlockSpec((bm, r), lambda j, i: (i, 0)),
            pl.BlockSpec((r, hb * width), lambda j, i: (0, j)),
            pl.BlockSpec((bm, LANES), lambda j, i: (i, 0)),
        ],
        out_specs=[
            pl.BlockSpec((None, hb, bm, QK_NOPE + LANES),
                         lambda j, i: (i // per_seq, j, i % per_seq, 0)),
            pl.BlockSpec((None, hb, bm, V_DIM + LANES),
                         lambda j, i: (i // per_seq, j, i % per_seq, 0)),
        ],
        out_shape=[
            jax.ShapeDtypeStruct((batch, heads, seq, QK_NOPE + LANES), BF16),
            jax.ShapeDtypeStruct((batch, heads, seq, V_DIM + LANES), BF16),
        ],
        compiler_params=_params("arbitrary", "arbitrary"),
        name="kv_up",
    )(c_kv, w_ukv, k_rope)


def _attn_kernel(q_ref, k_ref, v_ref, o_ref, m_sc, acc_sc, *, hb, tq):
    qi = pl.program_id(2)
    nt_dims = (((1,), (1,)), ((), ()))
    m_sc[...] = jnp.full(m_sc.shape, NEG_BIG, F32)
    acc_sc[...] = jnp.zeros_like(acc_sc)
    qs = [q_ref[j] for j in range(hb)]

    def k_tile(t, carry):
        k0 = pl.multiple_of(t * tq, tq)
        ss = [lax.dot_general(qs[j], k_ref[j, pl.ds(k0, tq), :], nt_dims,
                              preferred_element_type=F32) for j in range(hb)]
        ps = []
        for j in range(hb):
            m_old = m_sc[j]
            m_new = jnp.maximum(m_old, jnp.max(ss[j], axis=-1, keepdims=True))
            m_sc[j] = m_new
            p = jnp.exp2(ss[j] - jnp.concatenate([m_new] * (tq // LANES), axis=1))
            ps.append((jnp.exp2(m_old - m_new), p.astype(BF16)))
        for j in range(hb):
            alpha, p = ps[j]
            pv = jnp.dot(p, v_ref[j, pl.ds(k0, tq), :], preferred_element_type=F32)
            acc_sc[j] = jnp.concatenate([alpha] * ((V_DIM + LANES) // LANES), axis=1) * acc_sc[j] + pv
        return carry

    lax.fori_loop(0, qi + 1, k_tile, 0)
    for j in range(hb):
        acc = acc_sc[j]
        o_ref[:, j * V_DIM:(j + 1) * V_DIM] = (acc[:, :V_DIM] / acc[:, V_DIM:]).astype(BF16)


def _attention(q, k, v):
    batch, heads, seq, width = q.shape
    assert seq // CHUNK <= LANES - QK_ROPE, "chunk code needs one spare lane per chunk"
    hb = _tile(heads, 4)
    tq = _tile(seq, 512)
    return pl.pallas_call(
        functools.partial(_attn_kernel, hb=hb, tq=tq),
        grid=(batch, heads // hb, seq // tq),
        in_specs=[
            pl.BlockSpec((None, hb, tq, width), lambda b, h, i: (b, h, i, 0)),
            pl.BlockSpec((None, hb, seq, width), lambda b, h, i: (b, h, 0, 0)),
            pl.BlockSpec((None, hb, seq, V_DIM + LANES), lambda b, h, i: (b, h, 0, 0)),
        ],
        out_specs=pl.BlockSpec((None, tq, hb * V_DIM), lambda b, h, i: (b, i, h)),
        out_shape=jax.ShapeDtypeStruct((batch, seq, heads * V_DIM), BF16),
        scratch_shapes=[pltpu.VMEM((hb, tq, LANES), F32),
                        pltpu.VMEM((hb, tq, V_DIM + LANES), F32)],
        compiler_params=_params("arbitrary", "arbitrary", "arbitrary"),
        name="attention",
    )(q, k, v)


def _final_norm_kernel(x_ref, g_ref, o_ref):
    x = x_ref[...]
    o_ref[...] = x * lax.rsqrt(jnp.mean(x * x, axis=-1, keepdims=True) + EPS) * g_ref[...]


def _final_norm(x2, g):
    m, d = x2.shape
    ts = _tile(m, 256)
    return pl.pallas_call(
        _final_norm_kernel,
        grid=(m // ts,),
        in_specs=[pl.BlockSpec((ts, d), lambda i: (i, 0)),
                  pl.BlockSpec((1, d), lambda i: (0, 0))],
        out_specs=pl.BlockSpec((ts, d), lambda i: (i, 0)),
        out_shape=jax.ShapeDtypeStruct((m, d), F32),
        compiler_params=_params("arbitrary"),
        name="final_norm",
    )(x2, g.reshape(1, d))


def _pad_mla_w_in(w):
    return jnp.pad(w, ((0, 0), (0, LANES - QK_ROPE))).astype(BF16)


def _pad_w_uq(w):
    r = w.shape[0]
    heads = w.shape[1] // (QK_NOPE + QK_ROPE)
    w3 = w.reshape(r, heads, QK_NOPE + QK_ROPE)
    w3 = jnp.pad(w3, ((0, 0), (0, 0), (0, LANES - QK_ROPE)))
    return w3.reshape(r, heads * (QK_NOPE + LANES)).astype(BF16)


def kernel(x, c, positions, ada_w, ada_b, sgu_w_in, sgu_b_in, sgu_ln_g, sgu_ln_b, sgu_w_s, sgu_b_s, sgu_w_out, mla_w_in, mla_g_q, mla_g_kv, mla_w_uq, mla_w_ukv, mla_w_o, ffn_w_up, ffn_conv_w, ffn_conv_b, ffn_w_down, norm_g):
    batch, seq, d = x.shape
    depth = ada_w.shape[0]
    m = batch * seq

    c_rows = 2 * SUBLANES
    c_pad = jnp.pad(c, ((0, c_rows - batch), (0, 0)))
    mod = _ada_mod(c_pad, ada_w, ada_b)[:, :batch]
    mod4 = mod.reshape(depth, batch, N_MOD, d)

    inv_freq = ROPE_THETA ** (-jnp.arange(0, QK_ROPE, 2, dtype=F32) / QK_ROPE)
    freq_row = jnp.concatenate([inv_freq, inv_freq, jnp.zeros((LANES - QK_ROPE,), F32)]).reshape(1, LANES)
    cos_t, sin_t = _rope_tables(positions.reshape(m, 1), freq_row)

    x2 = x.reshape(m, d)
    for i in range(depth):
        j = i // 2
        h = _norm_mod(x2, mod4, i, 0, 1, seq)
        if i % 2 == 0:
            z = _sgu_in(h, sgu_w_in, j, sgu_b_in[j])
            gated = _sgu_gate(z, sgu_ln_g[j], sgu_ln_b[j], sgu_w_s[j], sgu_b_s[j])
            x2 = _mm_residual(gated, sgu_w_out[j].astype(BF16), x2, mod4, i, 2, seq)
        else:
            c_q, c_kv, k_rope = _mla_in(h, _pad_mla_w_in(mla_w_in[j]), mla_g_q[j], mla_g_kv[j],
                                        cos_t, sin_t, seq)
            q = _q_up(c_q, _pad_w_uq(mla_w_uq[j]), cos_t, sin_t, batch, seq)
            k, v = _kv_up(c_kv, mla_w_ukv[j].astype(BF16), k_rope, batch, seq)
            o = _attention(q, k, v).reshape(m, -1)
            x2 = _mm_residual(o, mla_w_o[j].astype(BF16), x2, mod4, i, 2, seq)
        h = _norm_mod(x2, mod4, i, 3, 4, seq)
        act = _ffn_up(h, ffn_w_up, i, ffn_conv_w[i], ffn_conv_b[i], seq)
        x2 = _mm_residual(act, ffn_w_down[i].astype(BF16), x2, mod4, i, 5, seq)
    return _final_norm(x2, norm_g).reshape(batch, seq, d)
```

```python
import functools
import math

import jax
import jax.numpy as jnp
from jax import lax
from jax.experimental import pallas as pl
from jax.experimental.pallas import tpu as pltpu

CHUNK = 64
EPS = 1e-6
N_MOD = 6
SGU_BLOCK = 128
SGU_GROUPS = 16
MLA_HEADS = 64
Q_LORA = 1536
KV_LORA = 512
QK_NOPE = 128
QK_ROPE = 64
V_DIM = 128
ROPE_THETA = 10000.0
CONV_W = 3

LANES = 128
SUBLANES = 8
VMEM_LIMIT_BYTES = 56 * 1024 * 1024

F32 = jnp.float32
BF16 = jnp.bfloat16
NEG_BIG = -1e30


def _params(*semantics):
    return pltpu.CompilerParams(dimension_semantics=semantics,
                                vmem_limit_bytes=VMEM_LIMIT_BYTES)


def _tile(dim, want):
    t = min(dim, want)
    while dim % t:
        t -= 1
    return t


def _ada_kernel(c_ref, w_ref, b_ref, o_ref):
    c = c_ref[...]
    c_act = (c / (1.0 + jnp.exp(-c))).astype(BF16)
    w = w_ref[...].astype(BF16)
    o_ref[...] = jnp.dot(c_act, w, preferred_element_type=F32) + b_ref[...]


def _ada_mod(c_pad, ada_w, ada_b):
    depth, d, n = ada_w.shape
    rows = c_pad.shape[0]
    bn = _tile(n, 512)
    return pl.pallas_call(
        _ada_kernel,
        grid=(depth, n // bn),
        in_specs=[
            pl.BlockSpec((rows, d), lambda i, j: (0, 0)),
            pl.BlockSpec((None, d, bn), lambda i, j: (i, 0, j)),
            pl.BlockSpec((None, 1, bn), lambda i, j: (i, 0, j)),
        ],
        out_specs=pl.BlockSpec((None, rows, bn), lambda i, j: (i, 0, j)),
        out_shape=jax.ShapeDtypeStruct((depth, rows, n), F32),
        compiler_params=_params("arbitrary", "arbitrary"),
        name="ada_mod",
    )(c_pad, ada_w, ada_b.reshape(depth, 1, n))


def _normmod_kernel(x_ref, mod_ref, o_ref, *, shift_row, scale_row):
    x = x_ref[...]
    ms = jnp.mean(x * x, axis=-1, keepdims=True)
    y = x * lax.rsqrt(ms + EPS)
    sh = mod_ref[shift_row:shift_row + 1, :]
    sc = mod_ref[scale_row:scale_row + 1, :]
    o_ref[...] = (y * (1.0 + sc) + sh).astype(BF16)


def _norm_mod(x2, mod4, layer, shift_row, scale_row, seq):
    m, d = x2.shape
    ts = _tile(seq, 256)
    per_seq = seq // ts
    return pl.pallas_call(
        functools.partial(_normmod_kernel, shift_row=shift_row, scale_row=scale_row),
        grid=(m // ts,),
        in_specs=[
            pl.BlockSpec((ts, d), lambda i: (i, 0)),
            pl.BlockSpec((None, None, N_MOD, d), lambda i: (layer, i // per_seq, 0, 0)),
        ],
        out_specs=pl.BlockSpec((ts, d), lambda i: (i, 0)),
        out_shape=jax.ShapeDtypeStruct((m, d), BF16),
        compiler_params=_params("arbitrary"),
        name="norm_mod",
    )(x2, mod4)


def _sgu_in_kernel(h_ref, w_ref, b_ref, z_ref, w_bf, *, n_col):
    p = pl.program_id(0)
    i = pl.program_id(1)
    w_rows = w_ref.shape[0]

    def cast_chunk():
        w_bf[p % 2, pl.ds(pl.multiple_of(i * w_rows, w_rows), w_rows), :] = w_ref[...].astype(BF16)

    def compute():
        a = jnp.dot(h_ref[...], w_bf[(p + 1) % 2], preferred_element_type=F32) + b_ref[...]
        z = 0.5 * a * (1.0 + lax.erf(a * math.sqrt(0.5)))
        z_ref[...] = z.astype(BF16)

    @pl.when(p == 0)
    def _():
        cast_chunk()

    @pl.when(jnp.logical_and(p > 0, p < n_col))
    def _():
        cast_chunk()
        compute()

    @pl.when(p == n_col)
    def _():
        compute()


def _sgu_in(h, w_all, layer, b):
    m, d = h.shape
    n = w_all.shape[2]
    bm = _tile(m, 1024)
    bn = _tile(n, 1024)
    nc = n // bn
    nm = m // bm
    w_rows = d // nm
    col = lambda p: jnp.maximum(p - 1, 0)
    return pl.pallas_call(
        functools.partial(_sgu_in_kernel, n_col=nc),
        grid=(nc + 1, nm),
        in_specs=[
            pl.BlockSpec((bm, d), lambda p, i: (jnp.where(p == 0, 0, i), 0)),
            pl.BlockSpec((None, w_rows, bn), lambda p, i: (layer, i, jnp.minimum(p, nc - 1))),
            pl.BlockSpec((1, bn), lambda p, i: (0, col(p))),
        ],
        out_specs=pl.BlockSpec((bm, bn), lambda p, i: (jnp.where(p == 0, 0, i), col(p))),
        out_shape=jax.ShapeDtypeStruct((m, n), BF16),
        scratch_shapes=[pltpu.VMEM((2, d, bn), BF16)],
        compiler_params=_params("arbitrary", "arbitrary"),
        name="sgu_in",
    )(h, w_all, b.reshape(1, n))


def _sgu_gate_kernel(u_ref, v_ref, g_ref, b_ref, ws_ref, bst_ref, o_ref, *, groups):
    blk, hidden = v_ref.shape
    gd = hidden // groups
    s1 = jnp.zeros((blk, 1), F32)
    s2 = jnp.zeros((blk, 1), F32)
    for g in range(groups):
        vg = v_ref[:, g * gd:(g + 1) * gd].astype(F32)
        s1 = s1 + jnp.sum(vg, axis=-1, keepdims=True)
    mu = s1 * (1.0 / hidden)
    for g in range(groups):
        vg = v_ref[:, g * gd:(g + 1) * gd].astype(F32) - mu
        s2 = s2 + jnp.sum(vg * vg, axis=-1, keepdims=True)
    rstd = lax.rsqrt(s2 * (1.0 / hidden) + EPS)
    t_out = lax.broadcasted_iota(jnp.int32, (blk, blk), 0) // CHUNK
    s_in = lax.broadcasted_iota(jnp.int32, (blk, blk), 1) // CHUNK
    causal = s_in <= t_out
    for g in range(groups):
        sl = slice(g * gd, (g + 1) * gd)
        vn = (v_ref[:, sl].astype(F32) - mu) * rstd * g_ref[:, sl] + b_ref[:, sl]
        w = jnp.where(causal, ws_ref[g], 0.0).astype(BF16)
        mixed = jnp.dot(w, vn.astype(BF16), preferred_element_type=F32) + bst_ref[:, g:g + 1]
        o_ref[:, sl] = (u_ref[:, sl].astype(F32) * mixed).astype(BF16)


def _sgu_gate(z, ln_g, ln_b, w_s, b_s):
    m, two_h = z.shape
    hidden = two_h // 2
    groups = w_s.shape[0]
    blk = SGU_BLOCK
    return pl.pallas_call(
        functools.partial(_sgu_gate_kernel, groups=groups),
        grid=(m // blk,),
        in_specs=[
            pl.BlockSpec((blk, hidden), lambda i: (i, 0)),
            pl.BlockSpec((blk, hidden), lambda i: (i, 1)),
            pl.BlockSpec((1, hidden), lambda i: (0, 0)),
            pl.BlockSpec((1, hidden), lambda i: (0, 0)),
            pl.BlockSpec((groups, blk, blk), lambda i: (0, 0, 0)),
            pl.BlockSpec((blk, groups), lambda i: (0, 0)),
        ],
        out_specs=pl.BlockSpec((blk, hidden), lambda i: (i, 0)),
        out_shape=jax.ShapeDtypeStruct((m, hidden), BF16),
        compiler_params=_params("arbitrary"),
        name="sgu_gate",
    )(z, z, ln_g.reshape(1, hidden), ln_b.reshape(1, hidden), w_s, b_s.T)


def _mm_res_kernel(a_ref, w_ref, x_ref, mod_ref, o_ref, *, gate_row, nk):
    k = pl.program_id(2)

    @pl.when(k == 0)
    def _():
        o_ref[...] = jnp.zeros_like(o_ref)

    o_ref[...] += jnp.dot(a_ref[...], w_ref[...], preferred_element_type=F32)

    @pl.when(k == nk - 1)
    def _():
        o_ref[...] = x_ref[...] + mod_ref[gate_row:gate_row + 1, :] * o_ref[...]


def _mm_residual(a, w, x2, mod4, layer, gate_row, seq):
    m, kdim = a.shape
    d = w.shape[1]
    bm = _tile(seq, 1024)
    bn = _tile(d, 2048)
    tk = _tile(kdim, 1024)
    nk = kdim // tk
    per_seq = seq // bm
    return pl.pallas_call(
        functools.partial(_mm_res_kernel, gate_row=gate_row, nk=nk),
        grid=(m // bm, d // bn, nk),
        in_specs=[
            pl.BlockSpec((bm, tk), lambda i, j, k: (i, k)),
            pl.BlockSpec((tk, bn), lambda i, j, k: (k, j)),
            pl.BlockSpec((bm, bn), lambda i, j, k: (i, j)),
            pl.BlockSpec((None, None, N_MOD, bn), lambda i, j, k: (layer, i // per_seq, 0, j)),
        ],
        out_specs=pl.BlockSpec((bm, bn), lambda i, j, k: (i, j)),
        out_shape=jax.ShapeDtypeStruct((m, d), F32),
        compiler_params=_params("arbitrary", "arbitrary", "arbitrary"),
        name="mm_residual",
    )(a, w, x2, mod4)


FFN_ROW_SPLIT = 4


def _ffn_up_kernel(h_ref, wg_ref, wu_ref, cwg_ref, cwu_ref, cbg_ref, cbu_ref, o_ref,
                   w_bf, a_sc, *, per_seq, n_col):
    p = pl.program_id(0)
    i = pl.program_id(1)
    bm = h_ref.shape[0]
    w_rows, bn = wg_ref.shape
    sub = bm // FFN_ROW_SPLIT
    head = slice(0, SUBLANES)
    tail = slice(sub, sub + SUBLANES)

    def cast_chunk():
        chunk = pl.ds(pl.multiple_of(i * w_rows, w_rows), w_rows)
        w_bf[p % 2, chunk, :bn] = wg_ref[...].astype(BF16)
        w_bf[p % 2, chunk, bn:] = wu_ref[...].astype(BF16)

    def compute():
        use = (p + 1) % 2

        @pl.when(i % per_seq == 0)
        def _():
            a_sc[0, head, :] = jnp.zeros((SUBLANES, 2 * bn), F32)

        cw = jnp.concatenate([cwg_ref[...], cwu_ref[...]], axis=1)
        cb = jnp.concatenate([cbg_ref[...], cbu_ref[...]], axis=1)

        def matmul(r):
            a_sc[r % 2, SUBLANES:, :] = jnp.dot(h_ref[r * sub:(r + 1) * sub, :], w_bf[use],
                                                preferred_element_type=F32)

        matmul(0)
        for r in range(FFN_ROW_SPLIT):
            slot = r % 2
            a_sc[(r + 1) % 2 if r + 1 < FFN_ROW_SPLIT else 0, head, :] = a_sc[slot, tail, :]
            if r + 1 < FFN_ROW_SPLIT:
                matmul(r + 1)
            c = (cb + cw[0:1, :] * a_sc[slot, SUBLANES - 2:SUBLANES - 2 + sub, :]
                 + cw[1:2, :] * a_sc[slot, SUBLANES - 1:SUBLANES - 1 + sub, :]
                 + cw[2:3, :] * a_sc[slot, SUBLANES:, :])
            g, u = c[:, :bn], c[:, bn:]
            o_ref[r * sub:(r + 1) * sub, :] = (g / (1.0 + jnp.exp(-g)) * u).astype(BF16)

    @pl.when(p == 0)
    def _():
        cast_chunk()

    @pl.when(jnp.logical_and(p > 0, p < n_col))
    def _():
        cast_chunk()
        compute()

    @pl.when(p == n_col)
    def _():
        compute()


def _ffn_up(h, w_up_all, layer, conv_w, conv_b, seq):
    m, d = h.shape
    ff = w_up_all.shape[2] // 2
    bm = _tile(seq, 1024)
    bn = _tile(ff, 512)
    nf = ff // bn
    nm = m // bm
    per_seq = seq // bm
    w_rows = d // nm
    cb = conv_b.reshape(1, 2 * ff)
    w_up = w_up_all
    w_col = lambda p: jnp.minimum(p, nf - 1)
    col = lambda p: jnp.maximum(p - 1, 0)
    return pl.pallas_call(
        functools.partial(_ffn_up_kernel, per_seq=per_seq, n_col=nf),
        grid=(nf + 1, nm),
        in_specs=[
            pl.BlockSpec((bm, d), lambda p, i: (jnp.where(p == 0, 0, i), 0)),
            pl.BlockSpec((None, w_rows, bn), lambda p, i: (layer, i, w_col(p))),
            pl.BlockSpec((None, w_rows, bn), lambda p, i: (layer, i, w_col(p) + nf)),
            pl.BlockSpec((CONV_W, bn), lambda p, i: (0, col(p))),
            pl.BlockSpec((CONV_W, bn), lambda p, i: (0, col(p) + nf)),
            pl.BlockSpec((1, bn), lambda p, i: (0, col(p))),
            pl.BlockSpec((1, bn), lambda p, i: (0, col(p) + nf)),
        ],
        out_specs=pl.BlockSpec((bm, bn), lambda p, i: (jnp.where(p == 0, 0, i), col(p))),
        out_shape=jax.ShapeDtypeStruct((m, ff), BF16),
        scratch_shapes=[pltpu.VMEM((2, d, 2 * bn), BF16),
                        pltpu.VMEM((2, bm // FFN_ROW_SPLIT + SUBLANES, 2 * bn), F32)],
        compiler_params=_params("arbitrary", "arbitrary"),
        name="ffn_up",
    )(h, w_up, w_up, conv_w, conv_w, cb, cb)


def _rope_table_kernel(pos_ref, freq_ref, cos_ref, sin_ref):
    ang = pos_ref[...].astype(F32) * freq_ref[...]
    lane = lax.broadcasted_iota(jnp.int32, ang.shape, 1)
    half = QK_ROPE // 2
    live = lane < QK_ROPE
    cos_ref[...] = jnp.where(live, jnp.cos(ang), 0.0)
    sn = jnp.sin(ang)
    sin_ref[...] = jnp.where(live, jnp.where(lane < half, -sn, sn), 0.0)


def _rope_tables(pos_col, freq_row):
    m = pos_col.shape[0]
    bm = _tile(m, 1024)
    return pl.pallas_call(
        _rope_table_kernel,
        grid=(m // bm,),
        in_specs=[pl.BlockSpec((bm, 1), lambda i: (i, 0)),
                  pl.BlockSpec((1, LANES), lambda i: (0, 0))],
        out_specs=[pl.BlockSpec((bm, LANES), lambda i: (i, 0)),
                   pl.BlockSpec((bm, LANES), lambda i: (i, 0))],
        out_shape=[jax.ShapeDtypeStruct((m, LANES), F32)] * 2,
        compiler_params=_params("arbitrary"),
        name="rope_tables",
    )(pos_col, freq_row)


def _rope_128(x, cos_t, sin_t):
    half = QK_ROPE // 2
    lane = lax.broadcasted_iota(jnp.int32, x.shape, 1)
    swapped = jnp.where(lane < half,
                        pltpu.roll(x, LANES - half, axis=1),
                        pltpu.roll(x, half, axis=1))
    return x * cos_t + swapped * sin_t


def _with_chunk_code(block, frame0, is_key):
    lane = lax.broadcasted_iota(jnp.int32, block.shape, 1) - QK_ROPE
    chunk = (frame0 + lax.broadcasted_iota(jnp.int32, block.shape, 0)) // CHUNK
    if is_key:
        code = jnp.where(lane == chunk, NEG_BIG, 0.0)
    else:
        code = jnp.where(lane > chunk, 1.0, 0.0)
    return jnp.where(lane >= 0, code, block)


def _mla_in_kernel(h_ref, w_ref, gq_ref, gkv_ref, cos_ref, sin_ref,
                   cq_ref, ckv_ref, kr_ref, acc_ref, *, nk, per_seq):
    k = pl.program_id(1)
    frame0 = (pl.program_id(0) % per_seq) * h_ref.shape[0]

    @pl.when(k == 0)
    def _():
        acc_ref[...] = jnp.zeros_like(acc_ref)

    acc_ref[...] += jnp.dot(h_ref[...], w_ref[...], preferred_element_type=F32)

    @pl.when(k == nk - 1)
    def _():
        def rms(v, g):
            return v * lax.rsqrt(jnp.mean(v * v, axis=-1, keepdims=True) + EPS) * g

        cq_ref[...] = rms(acc_ref[:, :Q_LORA], gq_ref[...]).astype(BF16)
        ckv_ref[...] = rms(acc_ref[:, Q_LORA:Q_LORA + KV_LORA], gkv_ref[...]).astype(BF16)
        kr = _rope_128(acc_ref[:, Q_LORA + KV_LORA:], cos_ref[...], sin_ref[...])
        kr_ref[...] = _with_chunk_code(kr, frame0, True).astype(BF16)


def _mla_in(h, w_pad, g_q, g_kv, cos_t, sin_t, seq):
    m, d = h.shape
    n = w_pad.shape[1]
    bm = _tile(seq, 512)
    tk = _tile(d, 1024)
    nk = d // tk
    return pl.pallas_call(
        functools.partial(_mla_in_kernel, nk=nk, per_seq=seq // bm),
        grid=(m // bm, nk),
        in_specs=[
            pl.BlockSpec((bm, tk), lambda i, k: (i, k)),
            pl.BlockSpec((tk, n), lambda i, k: (k, 0)),
            pl.BlockSpec((1, Q_LORA), lambda i, k: (0, 0)),
            pl.BlockSpec((1, KV_LORA), lambda i, k: (0, 0)),
            pl.BlockSpec((bm, LANES), lambda i, k: (i, 0)),
            pl.BlockSpec((bm, LANES), lambda i, k: (i, 0)),
        ],
        out_specs=[
            pl.BlockSpec((bm, Q_LORA), lambda i, k: (i, 0)),
            pl.BlockSpec((bm, KV_LORA), lambda i, k: (i, 0)),
            pl.BlockSpec((bm, LANES), lambda i, k: (i, 0)),
        ],
        out_shape=[
            jax.ShapeDtypeStruct((m, Q_LORA), BF16),
            jax.ShapeDtypeStruct((m, KV_LORA), BF16),
            jax.ShapeDtypeStruct((m, LANES), BF16),
        ],
        scratch_shapes=[pltpu.VMEM((bm, n), F32)],
        compiler_params=_params("arbitrary", "arbitrary"),
        name="mla_in",
    )(h, w_pad, g_q.reshape(1, Q_LORA), g_kv.reshape(1, KV_LORA), cos_t, sin_t)


def _q_up_kernel(cq_ref, wn_ref, wr_ref, cos_ref, sin_ref, q_ref, *, hb, scale, per_seq):
    cq = cq_ref[...]
    nope = jnp.dot(cq, wn_ref[...], preferred_element_type=F32)
    rope = jnp.dot(cq, wr_ref[...], preferred_element_type=F32)
    frame0 = (pl.program_id(1) % per_seq) * cq_ref.shape[0]
    cos_t = cos_ref[...]
    sin_t = sin_ref[...]
    for j in range(hb):
        pair = rope[:, (j // 2) * LANES:(j // 2 + 1) * LANES]
        mine = pair if j % 2 == 0 else pltpu.roll(pair, QK_ROPE, axis=1)
        roped = _rope_128(mine, cos_t, sin_t)
        q_ref[j, :, :QK_NOPE] = (nope[:, j * QK_NOPE:(j + 1) * QK_NOPE] * scale).astype(BF16)
        q_ref[j, :, QK_NOPE:] = _with_chunk_code(roped * scale, frame0, False).astype(BF16)


def _q_up(c_q, w_nope, w_rope, cos_t, sin_t, batch, seq):
    m, r = c_q.shape
    width = QK_NOPE + LANES
    heads = w_nope.shape[1] // QK_NOPE
    hb = _tile(heads, 4)
    assert hb % 2 == 0 and 2 * QK_ROPE == LANES, "two heads' rotary columns share one lane group"
    bm = _tile(seq, 1024)
    per_seq = seq // bm
    scale = (QK_NOPE + QK_ROPE) ** -0.5 * math.log2(math.e)
    return pl.pallas_call(
        functools.partial(_q_up_kernel, hb=hb, scale=scale, per_seq=per_seq),
        grid=(heads // hb, m // bm),
        in_specs=[
            pl.BlockSpec((bm, r), lambda j, i: (i, 0)),
            pl.BlockSpec((r, hb * QK_NOPE), lambda j, i: (0, j)),
            pl.BlockSpec((r, hb * QK_ROPE), lambda j, i: (0, j)),
            pl.BlockSpec((bm, LANES), lambda j, i: (i, 0)),
            pl.BlockSpec((bm, LANES), lambda j, i: (i, 0)),
        ],
        out_specs=pl.BlockSpec((None, hb, bm, width),
                               lambda j, i: (i // per_seq, j, i % per_seq, 0)),
        out_shape=jax.ShapeDtypeStruct((batch, heads, seq, width), BF16),
        compiler_params=_params("arbitrary", "arbitrary"),
        name="q_up",
    )(c_q, w_nope, w_rope, cos_t, sin_t)


def _kv_up_kernel(ckv_ref, w_ref, kr_ref, k_ref, v_ref, *, hb):
    acc = jnp.dot(ckv_ref[...], w_ref[...], preferred_element_type=F32)
    width = QK_NOPE + V_DIM
    kr = kr_ref[...]
    ones = jnp.ones((acc.shape[0], LANES), BF16)
    for j in range(hb):
        k_ref[j, :, :QK_NOPE] = acc[:, j * width:j * width + QK_NOPE].astype(BF16)
        k_ref[j, :, QK_NOPE:] = kr
        v_ref[j, :, :V_DIM] = acc[:, j * width + QK_NOPE:(j + 1) * width].astype(BF16)
        v_ref[j, :, V_DIM:] = ones


def _kv_up(c_kv, w_ukv, k_rope, batch, seq):
    m, r = c_kv.shape
    width = QK_NOPE + V_DIM
    heads = w_ukv.shape[1] // width
    hb = _tile(heads, 4)
    bm = _tile(seq, 1024)
    per_seq = seq // bm
    return pl.pallas_call(
        functools.partial(_kv_up_kernel, hb=hb),
        grid=(heads // hb, m // bm),
        in_specs=[
            pl.BlockSpec((bm, r), lambda j, i: (i, 0)),
            pl.BlockSpec((r, hb * width), lambda j, i: (0, j)),
            pl.BlockSpec((bm, LANES), lambda j, i: (i, 0)),
        ],
        out_specs=[
            pl.BlockSpec((None, hb, bm, QK_NOPE + LANES),
                         lambda j, i: (i // per_seq, j, i % per_seq, 0)),
            pl.BlockSpec((None, hb, bm, V_DIM + LANES),
                         lambda j, i: (i // per_seq, j, i % per_seq, 0)),
        ],
        out_shape=[
            jax.ShapeDtypeStruct((batch, heads, seq, QK_NOPE + LANES), BF16),
            jax.ShapeDtypeStruct((batch, heads, seq, V_DIM + LANES), BF16),
        ],
        compiler_params=_params("arbitrary", "arbitrary"),
        name="kv_up",
    )(c_kv, w_ukv, k_rope)


def _attn_kernel(q_ref, k_ref, v_ref, o_ref, m_sc, acc_sc, *, hb, tq, tk):
    qi = pl.program_id(2)
    nt_dims = (((1,), (1,)), ((), ()))
    m_sc[...] = jnp.full(m_sc.shape, NEG_BIG, F32)
    acc_sc[...] = jnp.zeros_like(acc_sc)
    qs = [q_ref[j] for j in range(hb)]

    def k_tile(t, carry):
        k0 = pl.multiple_of(t * tk, tk)
        ss = [lax.dot_general(qs[j], k_ref[j, pl.ds(k0, tk), :], nt_dims,
                              preferred_element_type=F32) for j in range(hb)]
        ps = []
        for j in range(hb):
            m_old = m_sc[j]
            m_new = jnp.maximum(m_old, jnp.max(ss[j], axis=-1, keepdims=True))
            m_sc[j] = m_new
            p = jnp.exp2(ss[j] - jnp.concatenate([m_new] * (tk // LANES), axis=1))
            ps.append((jnp.exp2(m_old - m_new), p.astype(BF16)))
        for j in range(hb):
            alpha, p = ps[j]
            pv = jnp.dot(p, v_ref[j, pl.ds(k0, tk), :], preferred_element_type=F32)
            acc_sc[j] = jnp.concatenate([alpha] * ((V_DIM + LANES) // LANES), axis=1) * acc_sc[j] + pv
        return carry

    lax.fori_loop(0, (qi + 1) * (tq // tk), k_tile, 0)
    for j in range(hb):
        acc = acc_sc[j]
        o_ref[:, j * V_DIM:(j + 1) * V_DIM] = (acc[:, :V_DIM] / acc[:, V_DIM:]).astype(BF16)


def _attention(q, k, v):
    batch, heads, seq, width = q.shape
    assert seq // CHUNK <= LANES - QK_ROPE, "chunk code needs one spare lane per chunk"
    hb = _tile(heads, 4)
    tq = _tile(seq, 512)
    tk = tq
    return pl.pallas_call(
        functools.partial(_attn_kernel, hb=hb, tq=tq, tk=tk),
        grid=(batch, heads // hb, seq // tq),
        in_specs=[
            pl.BlockSpec((None, hb, tq, width), lambda b, h, i: (b, h, i, 0)),
            pl.BlockSpec((None, hb, seq, width), lambda b, h, i: (b, h, 0, 0)),
            pl.BlockSpec((None, hb, seq, V_DIM + LANES), lambda b, h, i: (b, h, 0, 0)),
        ],
        out_specs=pl.BlockSpec((None, tq, hb * V_DIM), lambda b, h, i: (b, i, h)),
        out_shape=jax.ShapeDtypeStruct((batch, seq, heads * V_DIM), BF16),
        scratch_shapes=[pltpu.VMEM((hb, tq, LANES), F32),
                        pltpu.VMEM((hb, tq, V_DIM + LANES), F32)],
        compiler_params=_params("arbitrary", "arbitrary", "arbitrary"),
        name="attention",
    )(q, k, v)


def _final_norm_kernel(x_ref, g_ref, o_ref):
    x = x_ref[...]
    o_ref[...] = x * lax.rsqrt(jnp.mean(x * x, axis=-1, keepdims=True) + EPS) * g_ref[...]


def _final_norm(x2, g):
    m, d = x2.shape
    ts = _tile(m, 256)
    return pl.pallas_call(
        _final_norm_kernel,
        grid=(m // ts,),
        in_specs=[pl.BlockSpec((ts, d), lambda i: (i, 0)),
                  pl.BlockSpec((1, d), lambda i: (0, 0))],
        out_specs=pl.BlockSpec((ts, d), lambda i: (i, 0)),
        out_shape=jax.ShapeDtypeStruct((m, d), F32),
        compiler_params=_params("arbitrary"),
        name="final_norm",
    )(x2, g.reshape(1, d))


def _pad_mla_w_in(w):
    return jnp.pad(w, ((0, 0), (0, LANES - QK_ROPE))).astype(BF16)


def _split_w_uq(w):
    r = w.shape[0]
    heads = w.shape[1] // (QK_NOPE + QK_ROPE)
    w3 = w.reshape(r, heads, QK_NOPE + QK_ROPE)
    return (w3[:, :, :QK_NOPE].reshape(r, heads * QK_NOPE).astype(BF16),
            w3[:, :, QK_NOPE:].reshape(r, heads * QK_ROPE).astype(BF16))


def kernel(x, c, positions, ada_w, ada_b, sgu_w_in, sgu_b_in, sgu_ln_g, sgu_ln_b, sgu_w_s, sgu_b_s, sgu_w_out, mla_w_in, mla_g_q, mla_g_kv, mla_w_uq, mla_w_ukv, mla_w_o, ffn_w_up, ffn_conv_w, ffn_conv_b, ffn_w_down, norm_g):
    batch, seq, d = x.shape
    depth = ada_w.shape[0]
    m = batch * seq

    c_rows = 2 * SUBLANES
    c_pad = jnp.pad(c, ((0, c_rows - batch), (0, 0)))
    mod = _ada_mod(c_pad, ada_w, ada_b)[:, :batch]
    mod4 = mod.reshape(depth, batch, N_MOD, d)

    inv_freq = ROPE_THETA ** (-jnp.arange(0, QK_ROPE, 2, dtype=F32) / QK_ROPE)
    freq_row = jnp.concatenate([inv_freq, inv_freq, jnp.zeros((LANES - QK_ROPE,), F32)]).reshape(1, LANES)
    cos_t, sin_t = _rope_tables(positions.reshape(m, 1), freq_row)

    x2 = x.reshape(m, d)
    for i in range(depth):
        j = i // 2
        h = _norm_mod(x2, mod4, i, 0, 1, seq)
        if i % 2 == 0:
            z = _sgu_in(h, sgu_w_in, j, sgu_b_in[j])
            gated = _sgu_gate(z, sgu_ln_g[j], sgu_ln_b[j], sgu_w_s[j], sgu_b_s[j])
            x2 = _mm_residual(gated, sgu_w_out[j].astype(BF16), x2, mod4, i, 2, seq)
        else:
            c_q, c_kv, k_rope = _mla_in(h, _pad_mla_w_in(mla_w_in[j]), mla_g_q[j], mla_g_kv[j],
                                        cos_t, sin_t, seq)
            q = _q_up(c_q, *_split_w_uq(mla_w_uq[j]), cos_t, sin_t, batch, seq)
            k, v = _kv_up(c_kv, mla_w_ukv[j].astype(BF16), k_rope, batch, seq)
            o = _attention(q, k, v).reshape(m, -1)
            x2 = _mm_residual(o, mla_w_o[j].astype(BF16), x2, mod4, i, 2, seq)
        h = _norm_mod(x2, mod4, i, 3, 4, seq)
        act = _ffn_up(h, ffn_w_up, i, ffn_conv_w[i], ffn_conv_b[i], seq)
        x2 = _mm_residual(act, ffn_w_down[i].astype(BF16), x2, mod4, i, 5, seq)
    return _final_norm(x2, norm_g).reshape(batch, seq, d)
```

```python
import functools
import math

import jax
import jax.numpy as jnp
from jax import lax
from jax.experimental import pallas as pl
from jax.experimental.pallas import tpu as pltpu

CHUNK = 64
EPS = 1e-6
N_MOD = 6
SGU_BLOCK = 128
SGU_GROUPS = 16
MLA_HEADS = 64
Q_LORA = 1536
KV_LORA = 512
QK_NOPE = 128
QK_ROPE = 64
V_DIM = 128
ROPE_THETA = 10000.0
CONV_W = 3

LANES = 128
SUBLANES = 8
VMEM_LIMIT_BYTES = 56 * 1024 * 1024

F32 = jnp.float32
BF16 = jnp.bfloat16
NEG_BIG = -1e30


def _params(*semantics):
    return pltpu.CompilerParams(dimension_semantics=semantics,
                                vmem_limit_bytes=VMEM_LIMIT_BYTES)


def _tile(dim, want):
    t = min(dim, want)
    while dim % t:
        t -= 1
    return t


def _ada_kernel(c_ref, w_ref, b_ref, o_ref):
    c = c_ref[...]
    c_act = (c / (1.0 + jnp.exp(-c))).astype(BF16)
    w = w_ref[...].astype(BF16)
    o_ref[...] = jnp.dot(c_act, w, preferred_element_type=F32) + b_ref[...]


def _ada_mod(c_pad, ada_w, ada_b):
    depth, d, n = ada_w.shape
    rows = c_pad.shape[0]
    bn = _tile(n, 512)
    return pl.pallas_call(
        _ada_kernel,
        grid=(depth, n // bn),
        in_specs=[
            pl.BlockSpec((rows, d), lambda i, j: (0, 0)),
            pl.BlockSpec((None, d, bn), lambda i, j: (i, 0, j)),
            pl.BlockSpec((None, 1, bn), lambda i, j: (i, 0, j)),
        ],
        out_specs=pl.BlockSpec((None, rows, bn), lambda i, j: (i, 0, j)),
        out_shape=jax.ShapeDtypeStruct((depth, rows, n), F32),
        compiler_params=_params("arbitrary", "arbitrary"),
        name="ada_mod",
    )(c_pad, ada_w, ada_b.reshape(depth, 1, n))


def _normmod_kernel(x_ref, mod_ref, o_ref, *, shift_row, scale_row):
    x = x_ref[...]
    ms = jnp.mean(x * x, axis=-1, keepdims=True)
    y = x * lax.rsqrt(ms + EPS)
    sh = mod_ref[shift_row:shift_row + 1, :]
    sc = mod_ref[scale_row:scale_row + 1, :]
    o_ref[...] = (y * (1.0 + sc) + sh).astype(BF16)


def _norm_mod(x2, mod4, layer, shift_row, scale_row, seq):
    m, d = x2.shape
    ts = _tile(seq, 256)
    per_seq = seq // ts
    return pl.pallas_call(
        functools.partial(_normmod_kernel, shift_row=shift_row, scale_row=scale_row),
        grid=(m // ts,),
        in_specs=[
            pl.BlockSpec((ts, d), lambda i: (i, 0)),
            pl.BlockSpec((None, None, N_MOD, d), lambda i: (layer, i // per_seq, 0, 0)),
        ],
        out_specs=pl.BlockSpec((ts, d), lambda i: (i, 0)),
        out_shape=jax.ShapeDtypeStruct((m, d), BF16),
        compiler_params=_params("arbitrary"),
        name="norm_mod",
    )(x2, mod4)


def _sgu_in_kernel(h_ref, w_ref, b_ref, z_ref, w_bf, *, n_col):
    p = pl.program_id(0)
    i = pl.program_id(1)
    w_rows = w_ref.shape[0]

    def cast_chunk():
        w_bf[p % 2, pl.ds(pl.multiple_of(i * w_rows, w_rows), w_rows), :] = w_ref[...].astype(BF16)

    def compute():
        a = jnp.dot(h_ref[...], w_bf[(p + 1) % 2], preferred_element_type=F32) + b_ref[...]
        z = 0.5 * a * (1.0 + lax.erf(a * math.sqrt(0.5)))
        z_ref[...] = z.astype(BF16)

    @pl.when(p == 0)
    def _():
        cast_chunk()

    @pl.when(jnp.logical_and(p > 0, p < n_col))
    def _():
        cast_chunk()
        compute()

    @pl.when(p == n_col)
    def _():
        compute()


def _sgu_in(h, w_all, layer, b):
    m, d = h.shape
    n = w_all.shape[2]
    bm = _tile(m, 1024)
    bn = _tile(n, 1024)
    nc = n // bn
    nm = m // bm
    w_rows = d // nm
    col = lambda p: jnp.maximum(p - 1, 0)
    return pl.pallas_call(
        functools.partial(_sgu_in_kernel, n_col=nc),
        grid=(nc + 1, nm),
        in_specs=[
            pl.BlockSpec((bm, d), lambda p, i: (jnp.where(p == 0, 0, i), 0)),
            pl.BlockSpec((None, w_rows, bn), lambda p, i: (layer, i, jnp.minimum(p, nc - 1))),
            pl.BlockSpec((1, bn), lambda p, i: (0, col(p))),
        ],
        out_specs=pl.BlockSpec((bm, bn), lambda p, i: (jnp.where(p == 0, 0, i), col(p))),
        out_shape=jax.ShapeDtypeStruct((m, n), BF16),
        scratch_shapes=[pltpu.VMEM((2, d, bn), BF16)],
        compiler_params=_params("arbitrary", "arbitrary"),
        name="sgu_in",
    )(h, w_all, b.reshape(1, n))


def _sgu_gate_kernel(u_ref, v_ref, g_ref, b_ref, ws_ref, bst_ref, o_ref, *, groups):
    blk, hidden = v_ref.shape
    gd = hidden // groups
    s1 = jnp.zeros((blk, 1), F32)
    s2 = jnp.zeros((blk, 1), F32)
    for g in range(groups):
        vg = v_ref[:, g * gd:(g + 1) * gd].astype(F32)
        s1 = s1 + jnp.sum(vg, axis=-1, keepdims=True)
    mu = s1 * (1.0 / hidden)
    for g in range(groups):
        vg = v_ref[:, g * gd:(g + 1) * gd].astype(F32) - mu
        s2 = s2 + jnp.sum(vg * vg, axis=-1, keepdims=True)
    rstd = lax.rsqrt(s2 * (1.0 / hidden) + EPS)
    t_out = lax.broadcasted_iota(jnp.int32, (blk, blk), 0) // CHUNK
    s_in = lax.broadcasted_iota(jnp.int32, (blk, blk), 1) // CHUNK
    causal = s_in <= t_out
    for g in range(groups):
        sl = slice(g * gd, (g + 1) * gd)
        vn = (v_ref[:, sl].astype(F32) - mu) * rstd * g_ref[:, sl] + b_ref[:, sl]
        w = jnp.where(causal, ws_ref[g], 0.0).astype(BF16)
        mixed = jnp.dot(w, vn.astype(BF16), preferred_element_type=F32) + bst_ref[:, g:g + 1]
        o_ref[:, sl] = (u_ref[:, sl].astype(F32) * mixed).astype(BF16)


def _sgu_gate(z, ln_g, ln_b, w_s, b_s):
    m, two_h = z.shape
    hidden = two_h // 2
    groups = w_s.shape[0]
    blk = SGU_BLOCK
    return pl.pallas_call(
        functools.partial(_sgu_gate_kernel, groups=groups),
        grid=(m // blk,),
        in_specs=[
            pl.BlockSpec((blk, hidden), lambda i: (i, 0)),
            pl.BlockSpec((blk, hidden), lambda i: (i, 1)),
            pl.BlockSpec((1, hidden), lambda i: (0, 0)),
            pl.BlockSpec((1, hidden), lambda i: (0, 0)),
            pl.BlockSpec((groups, blk, blk), lambda i: (0, 0, 0)),
            pl.BlockSpec((blk, groups), lambda i: (0, 0)),
        ],
        out_specs=pl.BlockSpec((blk, hidden), lambda i: (i, 0)),
        out_shape=jax.ShapeDtypeStruct((m, hidden), BF16),
        compiler_params=_params("arbitrary"),
        name="sgu_gate",
    )(z, z, ln_g.reshape(1, hidden), ln_b.reshape(1, hidden), w_s, b_s.T)


def _mm_res_kernel(a_ref, w_ref, x_ref, mod_ref, o_ref, *, gate_row, nk):
    k = pl.program_id(2)

    @pl.when(k == 0)
    def _():
        o_ref[...] = jnp.zeros_like(o_ref)

    o_ref[...] += jnp.dot(a_ref[...], w_ref[...].astype(BF16), preferred_element_type=F32)

    @pl.when(k == nk - 1)
    def _():
        o_ref[...] = x_ref[...] + mod_ref[gate_row:gate_row + 1, :] * o_ref[...]


MXU_DEPTH = 256


def _mm_residual(a, w_all, w_layer, x2, mod4, layer, gate_row, seq):
    m, kdim = a.shape
    d = w_all.shape[2]
    bm = _tile(seq, 2048)
    bn = _tile(d, 1024)
    tk = MXU_DEPTH * _tile(kdim // MXU_DEPTH, 4) if kdim % MXU_DEPTH == 0 else _tile(kdim, 1024)
    nk = kdim // tk
    per_seq = seq // bm
    return pl.pallas_call(
        functools.partial(_mm_res_kernel, gate_row=gate_row, nk=nk),
        grid=(m // bm, d // bn, nk),
        in_specs=[
            pl.BlockSpec((bm, tk), lambda i, j, k: (i, k)),
            pl.BlockSpec((None, tk, bn), lambda i, j, k: (w_layer, k, j)),
            pl.BlockSpec((bm, bn), lambda i, j, k: (i, j)),
            pl.BlockSpec((None, None, N_MOD, bn), lambda i, j, k: (layer, i // per_seq, 0, j)),
        ],
        out_specs=pl.BlockSpec((bm, bn), lambda i, j, k: (i, j)),
        out_shape=jax.ShapeDtypeStruct((m, d), F32),
        compiler_params=_params("arbitrary", "arbitrary", "arbitrary"),
        name="mm_residual",
    )(a, w_all, x2, mod4)


FFN_ROW_SPLIT = 8


def _ffn_up_kernel(h_ref, wg_ref, wu_ref, cwg_ref, cwu_ref, cbg_ref, cbu_ref, o_ref,
                   w_bf, a_sc, *, per_seq, n_col):
    p = pl.program_id(0)
    i = pl.program_id(1)
    bm = h_ref.shape[0]
    w_rows, bn = wg_ref.shape
    sub = bm // FFN_ROW_SPLIT
    head = slice(0, SUBLANES)
    tail = slice(sub, sub + SUBLANES)

    def cast_chunk():
        chunk = pl.ds(pl.multiple_of(i * w_rows, w_rows), w_rows)
        w_bf[p % 2, chunk, :bn] = wg_ref[...].astype(BF16)
        w_bf[p % 2, chunk, bn:] = wu_ref[...].astype(BF16)

    def compute():
        use = (p + 1) % 2

        @pl.when(i % per_seq == 0)
        def _():
            a_sc[0, head, :] = jnp.zeros((SUBLANES, 2 * bn), F32)

        cw = jnp.concatenate([cwg_ref[...], cwu_ref[...]], axis=1)
        cb = jnp.concatenate([cbg_ref[...], cbu_ref[...]], axis=1)

        def matmul(r):
            a_sc[r % 2, SUBLANES:, :] = jnp.dot(h_ref[r * sub:(r + 1) * sub, :], w_bf[use],
                                                preferred_element_type=F32)

        matmul(0)
        for r in range(FFN_ROW_SPLIT):
            slot = r % 2
            a_sc[(r + 1) % 2 if r + 1 < FFN_ROW_SPLIT else 0, head, :] = a_sc[slot, tail, :]
            if r + 1 < FFN_ROW_SPLIT:
                matmul(r + 1)
            c = (cb + cw[0:1, :] * a_sc[slot, SUBLANES - 2:SUBLANES - 2 + sub, :]
                 + cw[1:2, :] * a_sc[slot, SUBLANES - 1:SUBLANES - 1 + sub, :]
                 + cw[2:3, :] * a_sc[slot, SUBLANES:, :])
            g, u = c[:, :bn], c[:, bn:]
            o_ref[r * sub:(r + 1) * sub, :] = (g / (1.0 + jnp.exp(-g)) * u).astype(BF16)

    @pl.when(p == 0)
    def _():
        cast_chunk()

    @pl.when(jnp.logical_and(p > 0, p < n_col))
    def _():
        cast_chunk()
        compute()

    @pl.when(p == n_col)
    def _():
        compute()


def _ffn_up(h, w_up_all, layer, conv_w, conv_b, seq):
    m, d = h.shape
    ff = w_up_all.shape[2] // 2
    bm = _tile(seq, 1024)
    bn = _tile(ff, 512)
    nf = ff // bn
    nm = m // bm
    per_seq = seq // bm
    w_rows = d // nm
    cb = conv_b.reshape(1, 2 * ff)
    w_up = w_up_all
    w_col = lambda p: jnp.minimum(p, nf - 1)
    col = lambda p: jnp.maximum(p - 1, 0)
    return pl.pallas_call(
        functools.partial(_ffn_up_kernel, per_seq=per_seq, n_col=nf),
        grid=(nf + 1, nm),
        in_specs=[
            pl.BlockSpec((bm, d), lambda p, i: (jnp.where(p == 0, 0, i), 0)),
            pl.BlockSpec((None, w_rows, bn), lambda p, i: (layer, i, w_col(p))),
            pl.BlockSpec((None, w_rows, bn), lambda p, i: (layer, i, w_col(p) + nf)),
            pl.BlockSpec((CONV_W, bn), lambda p, i: (0, col(p))),
            pl.BlockSpec((CONV_W, bn), lambda p, i: (0, col(p) + nf)),
            pl.BlockSpec((1, bn), lambda p, i: (0, col(p))),
            pl.BlockSpec((1, bn), lambda p, i: (0, col(p) + nf)),
        ],
        out_specs=pl.BlockSpec((bm, bn), lambda p, i: (jnp.where(p == 0, 0, i), col(p))),
        out_shape=jax.ShapeDtypeStruct((m, ff), BF16),
        scratch_shapes=[pltpu.VMEM((2, d, 2 * bn), BF16),
                        pltpu.VMEM((2, bm // FFN_ROW_SPLIT + SUBLANES, 2 * bn), F32)],
        compiler_params=_params("arbitrary", "arbitrary"),
        name="ffn_up",
    )(h, w_up, w_up, conv_w, conv_w, cb, cb)


def _rope_table_kernel(pos_ref, freq_ref, cos_ref, sin_ref):
    ang = pos_ref[...].astype(F32) * freq_ref[...]
    lane = lax.broadcasted_iota(jnp.int32, ang.shape, 1)
    half = QK_ROPE // 2
    live = lane < QK_ROPE
    cos_ref[...] = jnp.where(live, jnp.cos(ang), 0.0)
    sn = jnp.sin(ang)
    sin_ref[...] = jnp.where(live, jnp.where(lane < half, -sn, sn), 0.0)


def _rope_tables(pos_col, freq_row):
    m = pos_col.shape[0]
    bm = _tile(m, 1024)
    return pl.pallas_call(
        _rope_table_kernel,
        grid=(m // bm,),
        in_specs=[pl.BlockSpec((bm, 1), lambda i: (i, 0)),
                  pl.BlockSpec((1, LANES), lambda i: (0, 0))],
        out_specs=[pl.BlockSpec((bm, LANES), lambda i: (i, 0)),
                   pl.BlockSpec((bm, LANES), lambda i: (i, 0))],
        out_shape=[jax.ShapeDtypeStruct((m, LANES), F32)] * 2,
        compiler_params=_params("arbitrary"),
        name="rope_tables",
    )(pos_col, freq_row)


def _rope_128(x, cos_t, sin_t):
    half = QK_ROPE // 2
    lane = lax.broadcasted_iota(jnp.int32, x.shape, 1)
    swapped = jnp.where(lane < half,
                        pltpu.roll(x, LANES - half, axis=1),
                        pltpu.roll(x, half, axis=1))
    return x * cos_t + swapped * sin_t


def _with_chunk_code(block, frame0, is_key):
    lane = lax.broadcasted_iota(jnp.int32, block.shape, 1) - QK_ROPE
    chunk = (frame0 + lax.broadcasted_iota(jnp.int32, block.shape, 0)) // CHUNK
    if is_key:
        code = jnp.where(lane == chunk, NEG_BIG, 0.0)
    else:
        code = jnp.where(lane > chunk, 1.0, 0.0)
    return jnp.where(lane >= 0, code, block)


def _mla_in_kernel(h_ref, w_ref, gq_ref, gkv_ref, cos_ref, sin_ref,
                   cq_ref, ckv_ref, kr_ref, acc_ref, *, nk, per_seq):
    k = pl.program_id(1)
    frame0 = (pl.program_id(0) % per_seq) * h_ref.shape[0]

    @pl.when(k == 0)
    def _():
        acc_ref[...] = jnp.zeros_like(acc_ref)

    acc_ref[...] += jnp.dot(h_ref[...], w_ref[...], preferred_element_type=F32)

    @pl.when(k == nk - 1)
    def _():
        def rms(v, g):
            return v * lax.rsqrt(jnp.mean(v * v, axis=-1, keepdims=True) + EPS) * g

        cq_ref[...] = rms(acc_ref[:, :Q_LORA], gq_ref[...]).astype(BF16)
        ckv_ref[...] = rms(acc_ref[:, Q_LORA:Q_LORA + KV_LORA], gkv_ref[...]).astype(BF16)
        kr = _rope_128(acc_ref[:, Q_LORA + KV_LORA:], cos_ref[...], sin_ref[...])
        kr_ref[...] = _with_chunk_code(kr, frame0, True).astype(BF16)


def _mla_in(h, w_pad, g_q, g_kv, cos_t, sin_t, seq):
    m, d = h.shape
    n = w_pad.shape[1]
    bm = _tile(seq, 512)
    tk = _tile(d, 1024)
    nk = d // tk
    return pl.pallas_call(
        functools.partial(_mla_in_kernel, nk=nk, per_seq=seq // bm),
        grid=(m // bm, nk),
        in_specs=[
            pl.BlockSpec((bm, tk), lambda i, k: (i, k)),
            pl.BlockSpec((tk, n), lambda i, k: (k, 0)),
            pl.BlockSpec((1, Q_LORA), lambda i, k: (0, 0)),
            pl.BlockSpec((1, KV_LORA), lambda i, k: (0, 0)),
            pl.BlockSpec((bm, LANES), lambda i, k: (i, 0)),
            pl.BlockSpec((bm, LANES), lambda i, k: (i, 0)),
        ],
        out_specs=[
            pl.BlockSpec((bm, Q_LORA), lambda i, k: (i, 0)),
            pl.BlockSpec((bm, KV_LORA), lambda i, k: (i, 0)),
            pl.BlockSpec((bm, LANES), lambda i, k: (i, 0)),
        ],
        out_shape=[
            jax.ShapeDtypeStruct((m, Q_LORA), BF16),
            jax.ShapeDtypeStruct((m, KV_LORA), BF16),
            jax.ShapeDtypeStruct((m, LANES), BF16),
        ],
        scratch_shapes=[pltpu.VMEM((bm, n), F32)],
        compiler_params=_params("arbitrary", "arbitrary"),
        name="mla_in",
    )(h, w_pad, g_q.reshape(1, Q_LORA), g_kv.reshape(1, KV_LORA), cos_t, sin_t)


def _q_up_kernel(cq_ref, wn_ref, wr_ref, cos_ref, sin_ref, q_ref, *, hb, scale, per_seq):
    cq = cq_ref[...]
    nope = jnp.dot(cq, wn_ref[...], preferred_element_type=F32)
    rope = jnp.dot(cq, wr_ref[...], preferred_element_type=F32)
    frame0 = (pl.program_id(1) % per_seq) * cq_ref.shape[0]
    cos_t = cos_ref[...]
    sin_t = sin_ref[...]
    for j in range(hb):
        pair = rope[:, (j // 2) * LANES:(j // 2 + 1) * LANES]
        mine = pair if j % 2 == 0 else pltpu.roll(pair, QK_ROPE, axis=1)
        roped = _rope_128(mine, cos_t, sin_t)
        q_ref[j, :, :QK_NOPE] = (nope[:, j * QK_NOPE:(j + 1) * QK_NOPE] * scale).astype(BF16)
        q_ref[j, :, QK_NOPE:] = _with_chunk_code(roped * scale, frame0, False).astype(BF16)


def _q_up(c_q, w_nope, w_rope, cos_t, sin_t, batch, seq):
    m, r = c_q.shape
    width = QK_NOPE + LANES
    heads = w_nope.shape[1] // QK_NOPE
    hb = _tile(heads, 4)
    assert hb % 2 == 0 and 2 * QK_ROPE == LANES, "two heads' rotary columns share one lane group"
    bm = _tile(seq, 1024)
    per_seq = seq // bm
    scale = (QK_NOPE + QK_ROPE) ** -0.5 * math.log2(math.e)
    return pl.pallas_call(
        functools.partial(_q_up_kernel, hb=hb, scale=scale, per_seq=per_seq),
        grid=(heads // hb, m // bm),
        in_specs=[
            pl.BlockSpec((bm, r), lambda j, i: (i, 0)),
            pl.BlockSpec((r, hb * QK_NOPE), lambda j, i: (0, j)),
            pl.BlockSpec((r, hb * QK_ROPE), lambda j, i: (0, j)),
            pl.BlockSpec((bm, LANES), lambda j, i: (i, 0)),
            pl.BlockSpec((bm, LANES), lambda j, i: (i, 0)),
        ],
        out_specs=pl.BlockSpec((None, hb, bm, width),
                               lambda j, i: (i // per_seq, j, i % per_seq, 0)),
        out_shape=jax.ShapeDtypeStruct((batch, heads, seq, width), BF16),
        compiler_params=_params("arbitrary", "arbitrary"),
        name="q_up",
    )(c_q, w_nope, w_rope, cos_t, sin_t)


def _kv_up_kernel(ckv_ref, w_ref, kr_ref, k_ref, v_ref, *, hb):
    acc = jnp.dot(ckv_ref[...], w_ref[...], preferred_element_type=F32)
    width = QK_NOPE + V_DIM
    kr = kr_ref[...]
    ones = jnp.ones((acc.shape[0], LANES), BF16)
    for j in range(hb):
        k_ref[j, :, :QK_NOPE] = acc[:, j * width:j * width + QK_NOPE].astype(BF16)
        k_ref[j, :, QK_NOPE:] = kr
        v_ref[j, :, :V_DIM] = acc[:, j * width + QK_NOPE:(j + 1) * width].astype(BF16)
        v_ref[j, :, V_DIM:] = ones


def _kv_up(c_kv, w_ukv, k_rope, batch, seq):
    m, r = c_kv.shape
    width = QK_NOPE + V_DIM
    heads = w_ukv.shape[1] // width
    hb = _tile(heads, 4)
    bm = _tile(seq, 1024)
    per_seq = seq // bm
    return pl.pallas_call(
        functools.partial(_kv_up_kernel, hb=hb),
        grid=(heads // hb, m // bm),
        in_specs=[
            pl.BlockSpec((bm, r), lambda j, i: (i, 0)),
            pl.BlockSpec((r, hb * width), lambda j, i: (0, j)),
            pl.BlockSpec((bm, LANES), lambda j, i: (i, 0)),
        ],
        out_specs=[
            pl.BlockSpec((None, hb, bm, QK_NOPE + LANES),
                         lambda j, i: (i // per_seq, j, i % per_seq, 0)),
            pl.BlockSpec((None, hb, bm, V_DIM + LANES),
                         lambda j, i: (i // per_seq, j, i % per_seq, 0)),
        ],
        out_shape=[
            jax.ShapeDtypeStruct((batch, heads, seq, QK_NOPE + LANES), BF16),
            jax.ShapeDtypeStruct((batch, heads, seq, V_DIM + LANES), BF16),
        ],
        compiler_params=_params("arbitrary", "arbitrary"),
        name="kv_up",
    )(c_kv, w_ukv, k_rope)


def _attn_kernel(q_ref, k_ref, v_ref, o_ref, m_sc, acc_sc, *, hb, tq, tk):
    qi = pl.program_id(2)
    nt_dims = (((1,), (1,)), ((), ()))
    m_sc[...] = jnp.full(m_sc.shape, NEG_BIG, F32)
    acc_sc[...] = jnp.zeros_like(acc_sc)
    qs = [q_ref[j] for j in range(hb)]

    def k_tile(t, carry):
        k0 = pl.multiple_of(t * tk, tk)
        ss = [lax.dot_general(qs[j], k_ref[j, pl.ds(k0, tk), :], nt_dims,
                              preferred_element_type=F32) for j in range(hb)]
        ps = []
        for j in range(hb):
            m_old = m_sc[j]
            m_new = jnp.maximum(m_old, jnp.max(ss[j], axis=-1, keepdims=True))
            m_sc[j] = m_new
            p = jnp.exp2(ss[j] - jnp.concatenate([m_new] * (tk // LANES), axis=1))
            ps.append((jnp.exp2(m_old - m_new), p.astype(BF16)))
        for j in range(hb):
            alpha, p = ps[j]
            pv = jnp.dot(p, v_ref[j, pl.ds(k0, tk), :], preferred_element_type=F32)
            acc_sc[j] = jnp.concatenate([alpha] * ((V_DIM + LANES) // LANES), axis=1) * acc_sc[j] + pv
        return carry

    lax.fori_loop(0, (qi + 1) * (tq // tk), k_tile, 0)
    for j in range(hb):
        acc = acc_sc[j]
        o_ref[:, j * V_DIM:(j + 1) * V_DIM] = (acc[:, :V_DIM] / acc[:, V_DIM:]).astype(BF16)


def _attention(q, k, v):
    batch, heads, seq, width = q.shape
    assert seq // CHUNK <= LANES - QK_ROPE, "chunk code needs one spare lane per chunk"
    hb = _tile(heads, 4)
    tq = _tile(seq, 512)
    tk = tq
    return pl.pallas_call(
        functools.partial(_attn_kernel, hb=hb, tq=tq, tk=tk),
        grid=(batch, heads // hb, seq // tq),
        in_specs=[
            pl.BlockSpec((None, hb, tq, width), lambda b, h, i: (b, h, i, 0)),
            pl.BlockSpec((None, hb, seq, width), lambda b, h, i: (b, h, 0, 0)),
            pl.BlockSpec((None, hb, seq, V_DIM + LANES), lambda b, h, i: (b, h, 0, 0)),
        ],
        out_specs=pl.BlockSpec((None, tq, hb * V_DIM), lambda b, h, i: (b, i, h)),
        out_shape=jax.ShapeDtypeStruct((batch, seq, heads * V_DIM), BF16),
        scratch_shapes=[pltpu.VMEM((hb, tq, LANES), F32),
                        pltpu.VMEM((hb, tq, V_DIM + LANES), F32)],
        compiler_params=_params("arbitrary", "arbitrary", "arbitrary"),
        name="attention",
    )(q, k, v)


def _final_norm_kernel(x_ref, g_ref, o_ref):
    x = x_ref[...]
    o_ref[...] = x * lax.rsqrt(jnp.mean(x * x, axis=-1, keepdims=True) + EPS) * g_ref[...]


def _final_norm(x2, g):
    m, d = x2.shape
    ts = _tile(m, 256)
    return pl.pallas_call(
        _final_norm_kernel,
        grid=(m // ts,),
        in_specs=[pl.BlockSpec((ts, d), lambda i: (i, 0)),
                  pl.BlockSpec((1, d), lambda i: (0, 0))],
        out_specs=pl.BlockSpec((ts, d), lambda i: (i, 0)),
        out_shape=jax.ShapeDtypeStruct((m, d), F32),
        compiler_params=_params("arbitrary"),
        name="final_norm",
    )(x2, g.reshape(1, d))


def _pad_mla_w_in(w):
    return jnp.pad(w, ((0, 0), (0, LANES - QK_ROPE))).astype(BF16)


def _split_w_uq(w):
    r = w.shape[0]
    heads = w.shape[1] // (QK_NOPE + QK_ROPE)
    w3 = w.reshape(r, heads, QK_NOPE + QK_ROPE)
    return (w3[:, :, :QK_NOPE].reshape(r, heads * QK_NOPE).astype(BF16),
            w3[:, :, QK_NOPE:].reshape(r, heads * QK_ROPE).astype(BF16))


def kernel(x, c, positions, ada_w, ada_b, sgu_w_in, sgu_b_in, sgu_ln_g, sgu_ln_b, sgu_w_s, sgu_b_s, sgu_w_out, mla_w_in, mla_g_q, mla_g_kv, mla_w_uq, mla_w_ukv, mla_w_o, ffn_w_up, ffn_conv_w, ffn_conv_b, ffn_w_down, norm_g):
    batch, seq, d = x.shape
    depth = ada_w.shape[0]
    m = batch * seq

    c_rows = 2 * SUBLANES
    c_pad = jnp.pad(c, ((0, c_rows - batch), (0, 0)))
    mod = _ada_mod(c_pad, ada_w, ada_b)[:, :batch]
    mod4 = mod.reshape(depth, batch, N_MOD, d)

    inv_freq = ROPE_THETA ** (-jnp.arange(0, QK_ROPE, 2, dtype=F32) / QK_ROPE)
    freq_row = jnp.concatenate([inv_freq, inv_freq, jnp.zeros((LANES - QK_ROPE,), F32)]).reshape(1, LANES)
    cos_t, sin_t = _rope_tables(positions.reshape(m, 1), freq_row)

    x2 = x.reshape(m, d)
    for i in range(depth):
        j = i // 2
        h = _norm_mod(x2, mod4, i, 0, 1, seq)
        if i % 2 == 0:
            z = _sgu_in(h, sgu_w_in, j, sgu_b_in[j])
            gated = _sgu_gate(z, sgu_ln_g[j], sgu_ln_b[j], sgu_w_s[j], sgu_b_s[j])
            x2 = _mm_residual(gated, sgu_w_out, j, x2, mod4, i, 2, seq)
        else:
            c_q, c_kv, k_rope = _mla_in(h, _pad_mla_w_in(mla_w_in[j]), mla_g_q[j], mla_g_kv[j],
                                        cos_t, sin_t, seq)
            q = _q_up(c_q, *_split_w_uq(mla_w_uq[j]), cos_t, sin_t, batch, seq)
            k, v = _kv_up(c_kv, mla_w_ukv[j].astype(BF16), k_rope, batch, seq)
            o = _attention(q, k, v).reshape(m, -1)
            x2 = _mm_residual(o, mla_w_o, j, x2, mod4, i, 2, seq)
        h = _norm_mod(x2, mod4, i, 3, 4, seq)
        act = _ffn_up(h, ffn_w_up, i, ffn_conv_w[i], ffn_conv_b[i], seq)
        x2 = _mm_residual(act, ffn_w_down, i, x2, mod4, i, 5, seq)
    return _final_norm(x2, norm_g).reshape(batch, seq, d)
```

```python
import functools
import math

import jax
import jax.numpy as jnp
from jax import lax
from jax.experimental import pallas as pl
from jax.experimental.pallas import tpu as pltpu

CHUNK = 64
EPS = 1e-6
N_MOD = 6
SGU_BLOCK = 128
SGU_GROUPS = 16
MLA_HEADS = 64
Q_LORA = 1536
KV_LORA = 512
QK_NOPE = 128
QK_ROPE = 64
V_DIM = 128
ROPE_THETA = 10000.0
CONV_W = 3

LANES = 128
SUBLANES = 8
VMEM_LIMIT_BYTES = 56 * 1024 * 1024

F32 = jnp.float32
BF16 = jnp.bfloat16
NEG_BIG = -1e30


def _params(*semantics):
    return pltpu.CompilerParams(dimension_semantics=semantics,
                                vmem_limit_bytes=VMEM_LIMIT_BYTES)


def _tile(dim, want):
    t = min(dim, want)
    while dim % t:
        t -= 1
    return t


def _ada_kernel(c_ref, w_ref, b_ref, o_ref):
    c = c_ref[...]
    c_act = (c / (1.0 + jnp.exp(-c))).astype(BF16)
    w = w_ref[...].astype(BF16)
    o_ref[...] = jnp.dot(c_act, w, preferred_element_type=F32) + b_ref[...]


def _ada_mod(c_pad, ada_w, ada_b):
    depth, d, n = ada_w.shape
    rows = c_pad.shape[0]
    bn = _tile(n, 512)
    return pl.pallas_call(
        _ada_kernel,
        grid=(depth, n // bn),
        in_specs=[
            pl.BlockSpec((rows, d), lambda i, j: (0, 0)),
            pl.BlockSpec((None, d, bn), lambda i, j: (i, 0, j)),
            pl.BlockSpec((None, 1, bn), lambda i, j: (i, 0, j)),
        ],
        out_specs=pl.BlockSpec((None, rows, bn), lambda i, j: (i, 0, j)),
        out_shape=jax.ShapeDtypeStruct((depth, rows, n), F32),
        compiler_params=_params("arbitrary", "arbitrary"),
        name="ada_mod",
    )(c_pad, ada_w, ada_b.reshape(depth, 1, n))


def _normmod_kernel(x_ref, mod_ref, o_ref, *, shift_row, scale_row):
    x = x_ref[...]
    ms = jnp.mean(x * x, axis=-1, keepdims=True)
    y = x * lax.rsqrt(ms + EPS)
    sh = mod_ref[shift_row:shift_row + 1, :]
    sc = mod_ref[scale_row:scale_row + 1, :]
    o_ref[...] = (y * (1.0 + sc) + sh).astype(BF16)


def _norm_mod(x2, mod4, layer, shift_row, scale_row, seq):
    m, d = x2.shape
    ts = _tile(seq, 256)
    per_seq = seq // ts
    return pl.pallas_call(
        functools.partial(_normmod_kernel, shift_row=shift_row, scale_row=scale_row),
        grid=(m // ts,),
        in_specs=[
            pl.BlockSpec((ts, d), lambda i: (i, 0)),
            pl.BlockSpec((None, None, N_MOD, d), lambda i: (layer, i // per_seq, 0, 0)),
        ],
        out_specs=pl.BlockSpec((ts, d), lambda i: (i, 0)),
        out_shape=jax.ShapeDtypeStruct((m, d), BF16),
        compiler_params=_params("arbitrary"),
        name="norm_mod",
    )(x2, mod4)


def _sgu_in_kernel(h_ref, w_ref, b_ref, z_ref, w_bf, *, n_col):
    p = pl.program_id(0)
    i = pl.program_id(1)
    w_rows = w_ref.shape[0]

    def cast_chunk():
        w_bf[p % 2, pl.ds(pl.multiple_of(i * w_rows, w_rows), w_rows), :] = w_ref[...].astype(BF16)

    def compute():
        a = jnp.dot(h_ref[...], w_bf[(p + 1) % 2], preferred_element_type=F32) + b_ref[...]
        z = 0.5 * a * (1.0 + lax.erf(a * math.sqrt(0.5)))
        z_ref[...] = z.astype(BF16)

    @pl.when(p == 0)
    def _():
        cast_chunk()

    @pl.when(jnp.logical_and(p > 0, p < n_col))
    def _():
        cast_chunk()
        compute()

    @pl.when(p == n_col)
    def _():
        compute()


def _sgu_in(h, w_all, layer, b):
    m, d = h.shape
    n = w_all.shape[2]
    bm = _tile(m, 1024)
    bn = _tile(n, 1024)
    nc = n // bn
    nm = m // bm
    w_rows = d // nm
    col = lambda p: jnp.maximum(p - 1, 0)
    return pl.pallas_call(
        functools.partial(_sgu_in_kernel, n_col=nc),
        grid=(nc + 1, nm),
        in_specs=[
            pl.BlockSpec((bm, d), lambda p, i: (jnp.where(p == 0, 0, i), 0)),
            pl.BlockSpec((None, w_rows, bn), lambda p, i: (layer, i, jnp.minimum(p, nc - 1))),
            pl.BlockSpec((1, bn), lambda p, i: (0, col(p))),
        ],
        out_specs=pl.BlockSpec((bm, bn), lambda p, i: (jnp.where(p == 0, 0, i), col(p))),
        out_shape=jax.ShapeDtypeStruct((m, n), BF16),
        scratch_shapes=[pltpu.VMEM((2, d, bn), BF16)],
        compiler_params=_params("arbitrary", "arbitrary"),
        name="sgu_in",
    )(h, w_all, b.reshape(1, n))


def _sgu_gate_kernel(u_ref, v_ref, g_ref, b_ref, ws_ref, bst_ref, o_ref, *, groups):
    blk, hidden = v_ref.shape
    gd = hidden // groups
    s1 = jnp.zeros((blk, 1), F32)
    s2 = jnp.zeros((blk, 1), F32)
    for g in range(groups):
        vg = v_ref[:, g * gd:(g + 1) * gd].astype(F32)
        s1 = s1 + jnp.sum(vg, axis=-1, keepdims=True)
    mu = s1 * (1.0 / hidden)
    for g in range(groups):
        vg = v_ref[:, g * gd:(g + 1) * gd].astype(F32) - mu
        s2 = s2 + jnp.sum(vg * vg, axis=-1, keepdims=True)
    rstd = lax.rsqrt(s2 * (1.0 / hidden) + EPS)
    t_out = lax.broadcasted_iota(jnp.int32, (blk, blk), 0) // CHUNK
    s_in = lax.broadcasted_iota(jnp.int32, (blk, blk), 1) // CHUNK
    causal = s_in <= t_out
    for g in range(groups):
        sl = slice(g * gd, (g + 1) * gd)
        vn = (v_ref[:, sl].astype(F32) - mu) * rstd * g_ref[:, sl] + b_ref[:, sl]
        w = jnp.where(causal, ws_ref[g], 0.0).astype(BF16)
        mixed = jnp.dot(w, vn.astype(BF16), preferred_element_type=F32) + bst_ref[:, g:g + 1]
        o_ref[:, sl] = (u_ref[:, sl].astype(F32) * mixed).astype(BF16)


def _sgu_gate(z, ln_g, ln_b, w_s, b_s):
    m, two_h = z.shape
    hidden = two_h // 2
    groups = w_s.shape[0]
    blk = SGU_BLOCK
    return pl.pallas_call(
        functools.partial(_sgu_gate_kernel, groups=groups),
        grid=(m // blk,),
        in_specs=[
            pl.BlockSpec((blk, hidden), lambda i: (i, 0)),
            pl.BlockSpec((blk, hidden), lambda i: (i, 1)),
            pl.BlockSpec((1, hidden), lambda i: (0, 0)),
            pl.BlockSpec((1, hidden), lambda i: (0, 0)),
            pl.BlockSpec((groups, blk, blk), lambda i: (0, 0, 0)),
            pl.BlockSpec((blk, groups), lambda i: (0, 0)),
        ],
        out_specs=pl.BlockSpec((blk, hidden), lambda i: (i, 0)),
        out_shape=jax.ShapeDtypeStruct((m, hidden), BF16),
        compiler_params=_params("arbitrary"),
        name="sgu_gate",
    )(z, z, ln_g.reshape(1, hidden), ln_b.reshape(1, hidden), w_s, b_s.T)


def _mm_res_kernel(a_ref, w_ref, x_ref, mod_ref, o_ref, *, gate_row, nk):
    k = pl.program_id(2)

    @pl.when(k == 0)
    def _():
        o_ref[...] = jnp.zeros_like(o_ref)

    o_ref[...] += jnp.dot(a_ref[...], w_ref[...].astype(BF16), preferred_element_type=F32)

    @pl.when(k == nk - 1)
    def _():
        o_ref[...] = x_ref[...] + mod_ref[gate_row:gate_row + 1, :] * o_ref[...]


MXU_DEPTH = 256


def _mm_residual(a, w_all, w_layer, x2, mod4, layer, gate_row, seq):
    m, kdim = a.shape
    d = w_all.shape[2]
    bm = _tile(seq, 2048)
    bn = _tile(d, 1024)
    tk = MXU_DEPTH * _tile(kdim // MXU_DEPTH, 4) if kdim % MXU_DEPTH == 0 else _tile(kdim, 1024)
    nk = kdim // tk
    per_seq = seq // bm
    return pl.pallas_call(
        functools.partial(_mm_res_kernel, gate_row=gate_row, nk=nk),
        grid=(m // bm, d // bn, nk),
        in_specs=[
            pl.BlockSpec((bm, tk), lambda i, j, k: (i, k)),
            pl.BlockSpec((None, tk, bn), lambda i, j, k: (w_layer, k, j)),
            pl.BlockSpec((bm, bn), lambda i, j, k: (i, j)),
            pl.BlockSpec((None, None, N_MOD, bn), lambda i, j, k: (layer, i // per_seq, 0, j)),
        ],
        out_specs=pl.BlockSpec((bm, bn), lambda i, j, k: (i, j)),
        out_shape=jax.ShapeDtypeStruct((m, d), F32),
        compiler_params=_params("arbitrary", "arbitrary", "arbitrary"),
        name="mm_residual",
    )(a, w_all, x2, mod4)


FFN_ROW_SPLIT = 8


def _ffn_up_kernel(h_ref, wg_ref, wu_ref, cwg_ref, cwu_ref, cbg_ref, cbu_ref, o_ref,
                   w_bf, a_sc, *, per_seq, n_col):
    p = pl.program_id(0)
    i = pl.program_id(1)
    bm = h_ref.shape[0]
    w_rows, bn = wg_ref.shape
    sub = bm // FFN_ROW_SPLIT
    head = slice(0, SUBLANES)
    tail = slice(sub, sub + SUBLANES)

    def cast_chunk():
        chunk = pl.ds(pl.multiple_of(i * w_rows, w_rows), w_rows)
        w_bf[p % 2, chunk, :bn] = wg_ref[...].astype(BF16)
        w_bf[p % 2, chunk, bn:] = wu_ref[...].astype(BF16)

    def compute():
        use = (p + 1) % 2

        @pl.when(i % per_seq == 0)
        def _():
            a_sc[0, head, :] = jnp.zeros((SUBLANES, 2 * bn), F32)

        cw = jnp.concatenate([cwg_ref[...], cwu_ref[...]], axis=1)
        cb = jnp.concatenate([cbg_ref[...], cbu_ref[...]], axis=1)

        def matmul(r):
            a_sc[r % 2, SUBLANES:, :] = jnp.dot(h_ref[r * sub:(r + 1) * sub, :], w_bf[use],
                                                preferred_element_type=F32)

        matmul(0)
        for r in range(FFN_ROW_SPLIT):
            slot = r % 2
            a_sc[(r + 1) % 2 if r + 1 < FFN_ROW_SPLIT else 0, head, :] = a_sc[slot, tail, :]
            if r + 1 < FFN_ROW_SPLIT:
                matmul(r + 1)
            c = (cb + cw[0:1, :] * a_sc[slot, SUBLANES - 2:SUBLANES - 2 + sub, :]
                 + cw[1:2, :] * a_sc[slot, SUBLANES - 1:SUBLANES - 1 + sub, :]
                 + cw[2:3, :] * a_sc[slot, SUBLANES:, :])
            g, u = c[:, :bn], c[:, bn:]
            o_ref[r * sub:(r + 1) * sub, :] = (g / (1.0 + jnp.exp(-g)) * u).astype(BF16)

    @pl.when(p == 0)
    def _():
        cast_chunk()

    @pl.when(jnp.logical_and(p > 0, p < n_col))
    def _():
        cast_chunk()
        compute()

    @pl.when(p == n_col)
    def _():
        compute()


def _ffn_up(h, w_up_all, layer, conv_w, conv_b, seq):
    m, d = h.shape
    ff = w_up_all.shape[2] // 2
    bm = _tile(seq, 1024)
    bn = _tile(ff, 512)
    nf = ff // bn
    nm = m // bm
    per_seq = seq // bm
    w_rows = d // nm
    cb = conv_b.reshape(1, 2 * ff)
    w_up = w_up_all
    w_col = lambda p: jnp.minimum(p, nf - 1)
    col = lambda p: jnp.maximum(p - 1, 0)
    return pl.pallas_call(
        functools.partial(_ffn_up_kernel, per_seq=per_seq, n_col=nf),
        grid=(nf + 1, nm),
        in_specs=[
            pl.BlockSpec((bm, d), lambda p, i: (jnp.where(p == 0, 0, i), 0)),
            pl.BlockSpec((None, w_rows, bn), lambda p, i: (layer, i, w_col(p))),
            pl.BlockSpec((None, w_rows, bn), lambda p, i: (layer, i, w_col(p) + nf)),
            pl.BlockSpec((CONV_W, bn), lambda p, i: (0, col(p))),
            pl.BlockSpec((CONV_W, bn), lambda p, i: (0, col(p) + nf)),
            pl.BlockSpec((1, bn), lambda p, i: (0, col(p))),
            pl.BlockSpec((1, bn), lambda p, i: (0, col(p) + nf)),
        ],
        out_specs=pl.BlockSpec((bm, bn), lambda p, i: (jnp.where(p == 0, 0, i), col(p))),
        out_shape=jax.ShapeDtypeStruct((m, ff), BF16),
        scratch_shapes=[pltpu.VMEM((2, d, 2 * bn), BF16),
                        pltpu.VMEM((2, bm // FFN_ROW_SPLIT + SUBLANES, 2 * bn), F32)],
        compiler_params=_params("arbitrary", "arbitrary"),
        name="ffn_up",
    )(h, w_up, w_up, conv_w, conv_w, cb, cb)


def _rope_table_kernel(pos_ref, freq_ref, cos_ref, sin_ref):
    ang = pos_ref[...].astype(F32) * freq_ref[...]
    lane = lax.broadcasted_iota(jnp.int32, ang.shape, 1)
    half = QK_ROPE // 2
    live = lane < QK_ROPE
    cos_ref[...] = jnp.where(live, jnp.cos(ang), 0.0)
    sn = jnp.sin(ang)
    sin_ref[...] = jnp.where(live, jnp.where(lane < half, -sn, sn), 0.0)


def _rope_tables(pos_col, freq_row):
    m = pos_col.shape[0]
    bm = _tile(m, 1024)
    return pl.pallas_call(
        _rope_table_kernel,
        grid=(m // bm,),
        in_specs=[pl.BlockSpec((bm, 1), lambda i: (i, 0)),
                  pl.BlockSpec((1, LANES), lambda i: (0, 0))],
        out_specs=[pl.BlockSpec((bm, LANES), lambda i: (i, 0)),
                   pl.BlockSpec((bm, LANES), lambda i: (i, 0))],
        out_shape=[jax.ShapeDtypeStruct((m, LANES), F32)] * 2,
        compiler_params=_params("arbitrary"),
        name="rope_tables",
    )(pos_col, freq_row)


def _rope_128(x, cos_t, sin_t):
    half = QK_ROPE // 2
    lane = lax.broadcasted_iota(jnp.int32, x.shape, 1)
    swapped = jnp.where(lane < half,
                        pltpu.roll(x, LANES - half, axis=1),
                        pltpu.roll(x, half, axis=1))
    return x * cos_t + swapped * sin_t


def _with_chunk_code(block, frame0, is_key):
    lane = lax.broadcasted_iota(jnp.int32, block.shape, 1) - QK_ROPE
    chunk = (frame0 + lax.broadcasted_iota(jnp.int32, block.shape, 0)) // CHUNK
    if is_key:
        code = jnp.where(lane == chunk, NEG_BIG, 0.0)
    else:
        code = jnp.where(lane > chunk, 1.0, 0.0)
    return jnp.where(lane >= 0, code, block)


def _mla_in_kernel(h_ref, w_ref, gq_ref, gkv_ref, cos_ref, sin_ref,
                   cq_ref, ckv_ref, kr_ref, acc_ref, *, nk, per_seq):
    k = pl.program_id(1)
    frame0 = (pl.program_id(0) % per_seq) * h_ref.shape[0]

    @pl.when(k == 0)
    def _():
        acc_ref[...] = jnp.zeros_like(acc_ref)

    acc_ref[...] += jnp.dot(h_ref[...], w_ref[...], preferred_element_type=F32)

    @pl.when(k == nk - 1)
    def _():
        def rms(v, g):
            return v * lax.rsqrt(jnp.mean(v * v, axis=-1, keepdims=True) + EPS) * g

        cq_ref[...] = rms(acc_ref[:, :Q_LORA], gq_ref[...]).astype(BF16)
        ckv_ref[...] = rms(acc_ref[:, Q_LORA:Q_LORA + KV_LORA], gkv_ref[...]).astype(BF16)
        kr = _rope_128(acc_ref[:, Q_LORA + KV_LORA:], cos_ref[...], sin_ref[...])
        kr_ref[...] = _with_chunk_code(kr, frame0, True).astype(BF16)


def _mla_in(h, w_pad, g_q, g_kv, cos_t, sin_t, seq):
    m, d = h.shape
    n = w_pad.shape[1]
    bm = _tile(seq, 512)
    tk = _tile(d, 1024)
    nk = d // tk
    return pl.pallas_call(
        functools.partial(_mla_in_kernel, nk=nk, per_seq=seq // bm),
        grid=(m // bm, nk),
        in_specs=[
            pl.BlockSpec((bm, tk), lambda i, k: (i, k)),
            pl.BlockSpec((tk, n), lambda i, k: (k, 0)),
            pl.BlockSpec((1, Q_LORA), lambda i, k: (0, 0)),
            pl.BlockSpec((1, KV_LORA), lambda i, k: (0, 0)),
            pl.BlockSpec((bm, LANES), lambda i, k: (i, 0)),
            pl.BlockSpec((bm, LANES), lambda i, k: (i, 0)),
        ],
        out_specs=[
            pl.BlockSpec((bm, Q_LORA), lambda i, k: (i, 0)),
            pl.BlockSpec((bm, KV_LORA), lambda i, k: (i, 0)),
            pl.BlockSpec((bm, LANES), lambda i, k: (i, 0)),
        ],
        out_shape=[
            jax.ShapeDtypeStruct((m, Q_LORA), BF16),
            jax.ShapeDtypeStruct((m, KV_LORA), BF16),
            jax.ShapeDtypeStruct((m, LANES), BF16),
        ],
        scratch_shapes=[pltpu.VMEM((bm, n), F32)],
        compiler_params=_params("arbitrary", "arbitrary"),
        name="mla_in",
    )(h, w_pad, g_q.reshape(1, Q_LORA), g_kv.reshape(1, KV_LORA), cos_t, sin_t)


def _q_up_kernel(cq_ref, wn_ref, wr_ref, cos_ref, sin_ref, q_ref, *, hb, scale, per_seq):
    cq = cq_ref[...]
    nope = jnp.dot(cq, wn_ref[...], preferred_element_type=F32)
    rope = jnp.dot(cq, wr_ref[...], preferred_element_type=F32)
    frame0 = (pl.program_id(1) % per_seq) * cq_ref.shape[0]
    cos_t = cos_ref[...]
    sin_t = sin_ref[...]
    for j in range(hb):
        pair = rope[:, (j // 2) * LANES:(j // 2 + 1) * LANES]
        mine = pair if j % 2 == 0 else pltpu.roll(pair, QK_ROPE, axis=1)
        roped = _rope_128(mine, cos_t, sin_t)
        q_ref[j, :, :QK_NOPE] = (nope[:, j * QK_NOPE:(j + 1) * QK_NOPE] * scale).astype(BF16)
        q_ref[j, :, QK_NOPE:] = _with_chunk_code(roped * scale, frame0, False).astype(BF16)


def _q_up(c_q, w_nope, w_rope, cos_t, sin_t, batch, seq):
    m, r = c_q.shape
    width = QK_NOPE + LANES
    heads = w_nope.shape[1] // QK_NOPE
    hb = _tile(heads, 4)
    assert hb % 2 == 0 and 2 * QK_ROPE == LANES, "two heads' rotary columns share one lane group"
    bm = _tile(seq, 1024)
    per_seq = seq // bm
    scale = (QK_NOPE + QK_ROPE) ** -0.5 * math.log2(math.e)
    return pl.pallas_call(
        functools.partial(_q_up_kernel, hb=hb, scale=scale, per_seq=per_seq),
        grid=(heads // hb, m // bm),
        in_specs=[
            pl.BlockSpec((bm, r), lambda j, i: (i, 0)),
            pl.BlockSpec((r, hb * QK_NOPE), lambda j, i: (0, j)),
            pl.BlockSpec((r, hb * QK_ROPE), lambda j, i: (0, j)),
            pl.BlockSpec((bm, LANES), lambda j, i: (i, 0)),
            pl.BlockSpec((bm, LANES), lambda j, i: (i, 0)),
        ],
        out_specs=pl.BlockSpec((None, hb, bm, width),
                               lambda j, i: (i // per_seq, j, i % per_seq, 0)),
        out_shape=jax.ShapeDtypeStruct((batch, heads, seq, width), BF16),
        compiler_params=_params("arbitrary", "arbitrary"),
        name="q_up",
    )(c_q, w_nope, w_rope, cos_t, sin_t)


def _kv_up_kernel(ckv_ref, w_ref, kr_ref, k_ref, v_ref, *, hb):
    acc = jnp.dot(ckv_ref[...], w_ref[...], preferred_element_type=F32)
    width = QK_NOPE + V_DIM
    kr = kr_ref[...]
    for j in range(hb):
        k_ref[j, :, :QK_NOPE] = acc[:, j * width:j * width + QK_NOPE].astype(BF16)
        k_ref[j, :, QK_NOPE:] = kr
        v_ref[j] = acc[:, j * width + QK_NOPE:(j + 1) * width].astype(BF16)


def _kv_up(c_kv, w_ukv, k_rope, batch, seq):
    m, r = c_kv.shape
    width = QK_NOPE + V_DIM
    heads = w_ukv.shape[1] // width
    hb = _tile(heads, 4)
    bm = _tile(seq, 1024)
    per_seq = seq // bm
    return pl.pallas_call(
        functools.partial(_kv_up_kernel, hb=hb),
        grid=(heads // hb, m // bm),
        in_specs=[
            pl.BlockSpec((bm, r), lambda j, i: (i, 0)),
            pl.BlockSpec((r, hb * width), lambda j, i: (0, j)),
            pl.BlockSpec((bm, LANES), lambda j, i: (i, 0)),
        ],
        out_specs=[
            pl.BlockSpec((None, hb, bm, QK_NOPE + LANES),
                         lambda j, i: (i // per_seq, j, i % per_seq, 0)),
            pl.BlockSpec((None, hb, bm, V_DIM),
                         lambda j, i: (i // per_seq, j, i % per_seq, 0)),
        ],
        out_shape=[
            jax.ShapeDtypeStruct((batch, heads, seq, QK_NOPE + LANES), BF16),
            jax.ShapeDtypeStruct((batch, heads, seq, V_DIM), BF16),
        ],
        compiler_params=_params("arbitrary", "arbitrary"),
        name="kv_up",
    )(c_kv, w_ukv, k_rope)


def _attn_kernel(q_ref, k_ref, v_ref, o_ref, m_sc, acc_sc, *, hb, tq, tk):
    qi = pl.program_id(2)
    nt_dims = (((1,), (1,)), ((), ()))
    m_sc[...] = jnp.full(m_sc.shape, NEG_BIG, F32)
    acc_sc[...] = jnp.zeros_like(acc_sc)
    qs = [q_ref[j] for j in range(hb)]

    def k_tile(t, carry):
        k0 = pl.multiple_of(t * tk, tk)
        ss = [lax.dot_general(qs[j], k_ref[j, pl.ds(k0, tk), :], nt_dims,
                              preferred_element_type=F32) for j in range(hb)]
        ps = []
        for j in range(hb):
            m_old = m_sc[j]
            m_new = jnp.maximum(m_old, jnp.max(ss[j], axis=-1, keepdims=True))
            m_sc[j] = m_new
            p = jnp.exp2(ss[j] - jnp.concatenate([m_new] * (tk // LANES), axis=1))
            ps.append((jnp.exp2(m_old - m_new), p.astype(BF16)))
        for j in range(hb):
            alpha, p = ps[j]
            v1 = jnp.concatenate([v_ref[j, pl.ds(k0, tk), :], jnp.ones((tk, LANES), BF16)], axis=1)
            pv = jnp.dot(p, v1, preferred_element_type=F32)
            acc_sc[j] = jnp.concatenate([alpha] * ((V_DIM + LANES) // LANES), axis=1) * acc_sc[j] + pv
        return carry

    lax.fori_loop(0, (qi + 1) * (tq // tk), k_tile, 0)
    for j in range(hb):
        acc = acc_sc[j]
        o_ref[:, j * V_DIM:(j + 1) * V_DIM] = (acc[:, :V_DIM] / acc[:, V_DIM:]).astype(BF16)


def _attention(q, k, v):
    batch, heads, seq, width = q.shape
    assert seq // CHUNK <= LANES - QK_ROPE, "chunk code needs one spare lane per chunk"
    hb = _tile(heads, 4)
    tq = _tile(seq, 512)
    tk = tq
    return pl.pallas_call(
        functools.partial(_attn_kernel, hb=hb, tq=tq, tk=tk),
        grid=(batch, heads // hb, seq // tq),
        in_specs=[
            pl.BlockSpec((None, hb, tq, width), lambda b, h, i: (b, h, i, 0)),
            pl.BlockSpec((None, hb, seq, width), lambda b, h, i: (b, h, 0, 0)),
            pl.BlockSpec((None, hb, seq, V_DIM), lambda b, h, i: (b, h, 0, 0)),
        ],
        out_specs=pl.BlockSpec((None, tq, hb * V_DIM), lambda b, h, i: (b, i, h)),
        out_shape=jax.ShapeDtypeStruct((batch, seq, heads * V_DIM), BF16),
        scratch_shapes=[pltpu.VMEM((hb, tq, LANES), F32),
                        pltpu.VMEM((hb, tq, V_DIM + LANES), F32)],
        compiler_params=_params("arbitrary", "arbitrary", "arbitrary"),
        name="attention",
    )(q, k, v)


def _final_norm_kernel(x_ref, g_ref, o_ref):
    x = x_ref[...]
    o_ref[...] = x * lax.rsqrt(jnp.mean(x * x, axis=-1, keepdims=True) + EPS) * g_ref[...]


def _final_norm(x2, g):
    m, d = x2.shape
    ts = _tile(m, 256)
    return pl.pallas_call(
        _final_norm_kernel,
        grid=(m // ts,),
        in_specs=[pl.BlockSpec((ts, d), lambda i: (i, 0)),
                  pl.BlockSpec((1, d), lambda i: (0, 0))],
        out_specs=pl.BlockSpec((ts, d), lambda i: (i, 0)),
        out_shape=jax.ShapeDtypeStruct((m, d), F32),
        compiler_params=_params("arbitrary"),
        name="final_norm",
    )(x2, g.reshape(1, d))


def _pad_mla_w_in(w):
    return jnp.pad(w, ((0, 0), (0, LANES - QK_ROPE))).astype(BF16)


def _split_w_uq(w):
    r = w.shape[0]
    heads = w.shape[1] // (QK_NOPE + QK_ROPE)
    w3 = w.reshape(r, heads, QK_NOPE + QK_ROPE)
    return (w3[:, :, :QK_NOPE].reshape(r, heads * QK_NOPE).astype(BF16),
            w3[:, :, QK_NOPE:].reshape(r, heads * QK_ROPE).astype(BF16))


def kernel(x, c, positions, ada_w, ada_b, sgu_w_in, sgu_b_in, sgu_ln_g, sgu_ln_b, sgu_w_s, sgu_b_s, sgu_w_out, mla_w_in, mla_g_q, mla_g_kv, mla_w_uq, mla_w_ukv, mla_w_o, ffn_w_up, ffn_conv_w, ffn_conv_b, ffn_w_down, norm_g):
    batch, seq, d = x.shape
    depth = ada_w.shape[0]
    m = batch * seq

    c_rows = 2 * SUBLANES
    c_pad = jnp.pad(c, ((0, c_rows - batch), (0, 0)))
    mod = _ada_mod(c_pad, ada_w, ada_b)[:, :batch]
    mod4 = mod.reshape(depth, batch, N_MOD, d)

    inv_freq = ROPE_THETA ** (-jnp.arange(0, QK_ROPE, 2, dtype=F32) / QK_ROPE)
    freq_row = jnp.concatenate([inv_freq, inv_freq, jnp.zeros((LANES - QK_ROPE,), F32)]).reshape(1, LANES)
    cos_t, sin_t = _rope_tables(positions.reshape(m, 1), freq_row)

    x2 = x.reshape(m, d)
    for i in range(depth):
        j = i // 2
        h = _norm_mod(x2, mod4, i, 0, 1, seq)
        if i % 2 == 0:
            z = _sgu_in(h, sgu_w_in, j, sgu_b_in[j])
            gated = _sgu_gate(z, sgu_ln_g[j], sgu_ln_b[j], sgu_w_s[j], sgu_b_s[j])
            x2 = _mm_residual(gated, sgu_w_out, j, x2, mod4, i, 2, seq)
        else:
            c_q, c_kv, k_rope = _mla_in(h, _pad_mla_w_in(mla_w_in[j]), mla_g_q[j], mla_g_kv[j],
                                        cos_t, sin_t, seq)
            q = _q_up(c_q, *_split_w_uq(mla_w_uq[j]), cos_t, sin_t, batch, seq)
            k, v = _kv_up(c_kv, mla_w_ukv[j].astype(BF16), k_rope, batch, seq)
            o = _attention(q, k, v).reshape(m, -1)
            x2 = _mm_residual(o, mla_w_o, j, x2, mod4, i, 2, seq)
        h = _norm_mod(x2, mod4, i, 3, 4, seq)
        act = _ffn_up(h, ffn_w_up, i, ffn_conv_w[i], ffn_conv_b[i], seq)
        x2 = _mm_residual(act, ffn_w_down, i, x2, mod4, i, 5, seq)
    return _final_norm(x2, norm_g).reshape(batch, seq, d)
```
